```python
import jax
import jax.numpy as jnp
from jax import lax
import numpy as np

D_MODEL = 1024
BATCH = 8
SEQ = 2048
DEPTH = 1
DEC_BATCH = 128
DEC_SEQ = 8
PAST_LEN = 16384
PAGE_SIZE = 128

RET_HEADS = 4
RET_DK = 128
RET_DV = 128
GLA_HEADS = 4
GLA_DK = 64
GLA_DV = 128
GLA_GATE_RANK = 16
GLA_GATE_NORM = 16.0
CHUNK = 64
ROPE_BASE = 10000.0
N_EXPERTS = 64
TOP_K = 8
N_GROUPS = 8
TOPK_GROUPS = 4
EXPERT_FF = 256
SHARED_FF = 256
ROUTED_SCALE = 2.5
MOE_BLOCK = 128
EPS = 1e-6

RET_QK_W = RET_HEADS * RET_DK
RET_V_W = RET_HEADS * RET_DV
GLA_QK_W = GLA_HEADS * GLA_DK
GLA_V_W = GLA_HEADS * GLA_DV
MIX_W = RET_V_W + GLA_V_W
IN_WIDTHS = (RET_QK_W, RET_QK_W, RET_V_W, RET_V_W,
             GLA_QK_W, GLA_QK_W, GLA_V_W, GLA_V_W, GLA_GATE_RANK)
IN_W = sum(IN_WIDTHS)
IN_SPLITS = tuple(int(s) for s in np.cumsum(IN_WIDTHS)[:-1])

kernel_name = "hybrid_retention_gla_moe_step"


def _rms_norm(x, g):
    xf = x.astype(jnp.float32)
    y = xf * lax.rsqrt(jnp.mean(xf * xf, axis=-1, keepdims=True) + EPS)
    return (y * g.astype(jnp.float32)).astype(x.dtype)


def _head_layer_norm(x, w, b):
    xf = x.astype(jnp.float32)
    mu = jnp.mean(xf, axis=-1, keepdims=True)
    var = jnp.mean(jnp.square(xf - mu), axis=-1, keepdims=True)
    y = (xf - mu) * lax.rsqrt(var + EPS) * w.astype(jnp.float32) + b.astype(jnp.float32)
    return y.astype(x.dtype)


def _rotary(x, pos):
    half = x.shape[-1] // 2
    inv_freq = ROPE_BASE ** (-jnp.arange(half, dtype=jnp.float32) / half)
    ang = pos[:, None] * inv_freq[None, :]
    cos = jnp.cos(ang)[None, :, None, :].astype(x.dtype)
    sin = jnp.sin(ang)[None, :, None, :].astype(x.dtype)
    x1, x2 = x[..., :half], x[..., half:]
    return jnp.concatenate([x1 * cos - x2 * sin, x1 * sin + x2 * cos], axis=-1)


def _to_chunks(a, c):
    b, t = a.shape[:2]
    n = -(-t // c)
    a = jnp.pad(a, [(0, 0), (0, n * c - t)] + [(0, 0)] * (a.ndim - 2))
    a = a.reshape((b, n, c) + a.shape[2:])
    return jnp.moveaxis(a, (1, 2), (0, 3))


def _from_chunks(o, t):
    o = jnp.moveaxis(o, (0, 3), (1, 2))
    b, n, c = o.shape[:3]
    return o.reshape((b, n * c) + o.shape[3:])[:, :t]


def _retention(q, k, v, log_gamma, s0):
    b_, t, h, _ = q.shape
    c = min(CHUNK, t)
    g = jnp.broadcast_to(log_gamma[None, None, :], (b_, t, h))
    qc, kc, vc, gc = (_to_chunks(a, c) for a in (q, k, v, g))
    causal = jnp.tril(jnp.ones((c, c), dtype=bool))

    def step(s, blk):
        qb, kb, vb, gb = blk
        qf, kf, vf = qb.astype(jnp.float32), kb.astype(jnp.float32), vb.astype(jnp.float32)
        b = jnp.cumsum(gb.astype(jnp.float32), axis=-1)
        diff = b[..., :, None] - b[..., None, :]
        dmat = jnp.where(causal, jnp.exp(jnp.where(causal, diff, 0.0)), 0.0)
        scores = jnp.einsum("bhid,bhjd->bhij", qf, kf) * dmat
        o = (jnp.einsum("bhij,bhjv->bhiv", scores, vf)
             + jnp.einsum("bhid,bhdv->bhiv", qf * jnp.exp(b)[..., None], s))
        last = b[..., -1:]
        s_new = (s * jnp.exp(last)[..., None]
                 + jnp.einsum("bhjd,bhjv->bhdv", kf * jnp.exp(last - b)[..., None], vf))
        return s_new, o.astype(v.dtype)

    s_fin, oc = lax.scan(step, s0.astype(jnp.float32), (qc, kc, vc, gc))
    return _from_chunks(oc, t), s_fin


def _gla(q, k, v, log_a, s0):
    t = q.shape[1]
    c = min(CHUNK, t)
    qc, kc, vc, gc = (_to_chunks(a, c) for a in (q, k, v, log_a))
    causal = jnp.tril(jnp.ones((c, c), dtype=bool))[:, :, None]

    def step(s, blk):
        qb, kb, vb, gb = blk
        qf, kf, vf = qb.astype(jnp.float32), kb.astype(jnp.float32), vb.astype(jnp.float32)
        b = jnp.cumsum(gb.astype(jnp.float32), axis=2)
        diff = b[:, :, :, None, :] - b[:, :, None, :, :]
        decay = jnp.where(causal, jnp.exp(jnp.where(causal, diff, 0.0)), 0.0)
        scores = jnp.einsum("bhid,bhjd,bhijd->bhij", qf, kf, decay)
        o = (jnp.einsum("bhij,bhjv->bhiv", scores, vf)
             + jnp.einsum("bhid,bhdv->bhiv", qf * jnp.exp(b), s))
        last = b[:, :, -1, :]
        s_new = (s * jnp.exp(last)[..., None]
                 + jnp.einsum("bhjd,bhjv->bhdv", kf * jnp.exp(last[:, :, None, :] - b), vf))
        return s_new, o.astype(v.dtype)

    s_fin, oc = lax.scan(step, s0.astype(jnp.float32), (qc, kc, vc, gc))
    return _from_chunks(oc, t), s_fin


def _mixer(h, pos, s_ret, s_gla, w_in, gla_gate_up, gla_gate_bias,
           ret_norm_w, ret_norm_b, gla_norm_w, w_out):
    b, t, _ = h.shape
    proj = h @ w_in
    rq, rk, rv, rg, gq, gk, gv, gg, ga = jnp.split(proj, IN_SPLITS, axis=-1)
    rq = _rotary(rq.reshape(b, t, RET_HEADS, RET_DK), pos)
    rk = _rotary(rk.reshape(b, t, RET_HEADS, RET_DK), pos) * (RET_DK ** -0.5)
    rv = rv.reshape(b, t, RET_HEADS, RET_DV)
    log_gamma = jnp.log(1.0 - 2.0 ** (-5.0 - jnp.arange(RET_HEADS, dtype=jnp.float32)))
    ro, s_ret_new = _retention(rq, rk, rv, log_gamma, s_ret)
    ro = _head_layer_norm(ro, ret_norm_w, ret_norm_b).reshape(b, t, RET_V_W)
    ro = jax.nn.silu(rg) * ro
    gq = gq.reshape(b, t, GLA_HEADS, GLA_DK) * (GLA_DK ** -0.5)
    gk = gk.reshape(b, t, GLA_HEADS, GLA_DK)
    gv = gv.reshape(b, t, GLA_HEADS, GLA_DV)
    gate_logit = (ga @ gla_gate_up + gla_gate_bias).astype(jnp.float32)
    log_a = (jax.nn.log_sigmoid(gate_logit) / GLA_GATE_NORM).reshape(b, t, GLA_HEADS, GLA_DK)
    go, s_gla_new = _gla(gq, gk, gv, log_a, s_gla)
    go = _rms_norm(go, gla_norm_w).reshape(b, t, GLA_V_W)
    go = jax.nn.silu(gg) * go
    out = jnp.concatenate([ro, go], axis=-1) @ w_out
    return out, s_ret_new, s_gla_new


def _swiglu(x, wg, wu, wd):
    return (jax.nn.silu(x @ wg) * (x @ wu)) @ wd


def _routed_experts(xf, idx, wts, w_gate, w_up, w_down):
    n, d = xf.shape
    m = n * TOP_K
    flat_e = idx.reshape(m)
    flat_tok = jnp.repeat(jnp.arange(n, dtype=jnp.int32), TOP_K)
    flat_w = wts.reshape(m)
    order = jnp.argsort(flat_e)
    se = flat_e[order]
    counts = jnp.bincount(flat_e, length=N_EXPERTS)
    start = jnp.cumsum(counts) - counts
    pcounts = (counts + MOE_BLOCK - 1) // MOE_BLOCK * MOE_BLOCK
    pend = jnp.cumsum(pcounts)
    pstart = pend - pcounts
    dest = pstart[se] + (jnp.arange(m, dtype=jnp.int32) - start[se])
    n_blocks = (m + N_EXPERTS * (MOE_BLOCK - 1) + MOE_BLOCK - 1) // MOE_BLOCK
    p = n_blocks * MOE_BLOCK
    buf_tok = jnp.full((p,), n, jnp.int32).at[dest].set(flat_tok[order])
    buf_w = jnp.zeros((p,), flat_w.dtype).at[dest].set(flat_w[order])
    block_start = jnp.arange(n_blocks, dtype=jnp.int32) * MOE_BLOCK
    block_e = jnp.minimum(jnp.sum(pend[None, :] <= block_start[:, None], axis=1), N_EXPERTS - 1)
    x_pad = jnp.concatenate([xf, jnp.zeros((1, d), xf.dtype)], axis=0)
    xb = x_pad[buf_tok].reshape(n_blocks, MOE_BLOCK, d)

    def run_block(args):
        xblk, e = args
        return _swiglu(xblk, w_gate[e], w_up[e], w_down[e])

    yb = lax.map(run_block, (xb, block_e)).reshape(p, d)
    yb = yb * buf_w[:, None].astype(yb.dtype)
    return jax.ops.segment_sum(yb, buf_tok, num_segments=n + 1)[:n]


def _moe(h, w_router, router_bias, w_exp_gate, w_exp_up, w_exp_down,
         w_sh_gate, w_sh_up, w_sh_down):
    b, t, d = h.shape
    xf = h.reshape(b * t, d)
    n = xf.shape[0]
    scores = jax.nn.sigmoid((xf @ w_router).astype(jnp.float32))
    sel = scores + router_bias.astype(jnp.float32)
    grp = sel.reshape(n, N_GROUPS, N_EXPERTS // N_GROUPS)
    grp_score = jnp.sum(lax.top_k(grp, 2)[0], axis=-1)
    _, top_g = lax.top_k(grp_score, TOPK_GROUPS)
    gmask = jnp.any(top_g[:, :, None] == jnp.arange(N_GROUPS)[None, None, :], axis=1)
    emask = jnp.repeat(gmask, N_EXPERTS // N_GROUPS, axis=1)
    _, idx = lax.top_k(jnp.where(emask, sel, -jnp.inf), TOP_K)
    w = jnp.take_along_axis(scores, idx, axis=-1)
    w = w / jnp.sum(w, axis=-1, keepdims=True) * ROUTED_SCALE
    routed = _routed_experts(xf, idx, w, w_exp_gate, w_exp_up, w_exp_down)
    shared = _swiglu(xf, w_sh_gate, w_sh_up, w_sh_down)
    return (routed + shared).reshape(b, t, d)


def _layer(x, c, pos, s_ret, s_gla, w_ada, b_ada, norm_pre_mix, norm_post_mix,
           norm_pre_ffn, norm_post_ffn, w_in, gla_gate_up, gla_gate_bias,
           ret_norm_w, ret_norm_b, gla_norm_w, w_out, w_router, router_bias,
           w_exp_gate, w_exp_up, w_exp_down, w_sh_gate, w_sh_up, w_sh_down):
    mod = jax.nn.silu(c) @ w_ada + b_ada
    sh1, sc1, g1, sh2, sc2, g2 = (m[:, None, :] for m in jnp.split(mod, 6, axis=-1))
    h = _rms_norm(x, norm_pre_mix) * (1.0 + sc1) + sh1
    mix, s_ret_new, s_gla_new = _mixer(h, pos, s_ret, s_gla, w_in, gla_gate_up, gla_gate_bias,
                                       ret_norm_w, ret_norm_b, gla_norm_w, w_out)
    x = x + g1 * _rms_norm(mix, norm_post_mix)
    h = _rms_norm(x, norm_pre_ffn) * (1.0 + sc2) + sh2
    f = _moe(h, w_router, router_bias, w_exp_gate, w_exp_up, w_exp_down,
             w_sh_gate, w_sh_up, w_sh_down)
    x = x + g2 * _rms_norm(f, norm_post_ffn)
    return x, s_ret_new, s_gla_new


def setup_inputs(seed: int = 0) -> dict:
    key = jax.random.key(seed)
    ks = jax.random.split(key, 32)
    f32 = jnp.float32

    def nrm(k, shape, scale):
        return jax.random.normal(k, shape, f32) * scale

    L = DEPTH
    D = D_MODEL
    return {
        "x_prompt": nrm(ks[0], (BATCH, SEQ, D), 1.0),
        "x_sample": nrm(ks[1], (DEC_BATCH, DEC_SEQ, D), 1.0),
        "state_ret": nrm(ks[2], (L, DEC_BATCH, RET_HEADS, RET_DK, RET_DV), 0.5),
        "state_gla": nrm(ks[3], (L, DEC_BATCH, GLA_HEADS, GLA_DK, GLA_DV), 1.0),
        "c_prompt": nrm(ks[4], (BATCH, D), 1.0),
        "c_sample": nrm(ks[5], (DEC_BATCH, D), 1.0),
        "w_ada": nrm(ks[6], (L, D, 6 * D), 0.2 * D ** -0.5),
        "b_ada": nrm(ks[7], (L, 6 * D), 0.02),
        "norm_pre_mix": 1.0 + nrm(ks[8], (L, D), 0.05),
        "norm_post_mix": 1.0 + nrm(ks[9], (L, D), 0.05),
        "norm_pre_ffn": 1.0 + nrm(ks[10], (L, D), 0.05),
        "norm_post_ffn": 1.0 + nrm(ks[11], (L, D), 0.05),
        "w_in": nrm(ks[12], (L, D, IN_W), D ** -0.5),
        "gla_gate_up": nrm(ks[13], (L, GLA_GATE_RANK, GLA_QK_W), GLA_GATE_RANK ** -0.5),
        "gla_gate_bias": nrm(ks[14], (L, GLA_QK_W), 0.1),
        "ret_norm_w": 1.0 + nrm(ks[15], (L, RET_HEADS, RET_DV), 0.05),
        "ret_norm_b": nrm(ks[16], (L, RET_HEADS, RET_DV), 0.02),
        "gla_norm_w": 1.0 + nrm(ks[17], (L, GLA_HEADS, GLA_DV), 0.05),
        "w_out": nrm(ks[18], (L, MIX_W, D), MIX_W ** -0.5),
        "w_router": nrm(ks[19], (L, D, N_EXPERTS), D ** -0.5),
        "router_bias": nrm(ks[20], (L, N_EXPERTS), 0.01),
        "w_exp_gate": nrm(ks[21], (L, N_EXPERTS, D, EXPERT_FF), D ** -0.5),
        "w_exp_up": nrm(ks[22], (L, N_EXPERTS, D, EXPERT_FF), D ** -0.5),
        "w_exp_down": nrm(ks[23], (L, N_EXPERTS, EXPERT_FF, D), EXPERT_FF ** -0.5),
        "w_sh_gate": nrm(ks[24], (L, D, SHARED_FF), D ** -0.5),
        "w_sh_up": nrm(ks[25], (L, D, SHARED_FF), D ** -0.5),
        "w_sh_down": nrm(ks[26], (L, SHARED_FF, D), SHARED_FF ** -0.5),
    }


def reference(x_prompt, x_sample, state_ret, state_gla, c_prompt, c_sample,
              w_ada, b_ada, norm_pre_mix, norm_post_mix, norm_pre_ffn, norm_post_ffn,
              w_in, gla_gate_up, gla_gate_bias, ret_norm_w, ret_norm_b, gla_norm_w, w_out,
              w_router, router_bias, w_exp_gate, w_exp_up, w_exp_down,
              w_sh_gate, w_sh_up, w_sh_down):
    bp, tp = x_prompt.shape[:2]
    ts = x_sample.shape[1]
    pos_p = jnp.arange(tp, dtype=jnp.float32)
    pos_s = PAST_LEN + jnp.arange(ts, dtype=jnp.float32)
    xp, xs = x_prompt, x_sample
    ret_p, gla_p, ret_s, gla_s = [], [], [], []
    for l in range(DEPTH):
        params = (w_ada[l], b_ada[l], norm_pre_mix[l], norm_post_mix[l], norm_pre_ffn[l],
                  norm_post_ffn[l], w_in[l], gla_gate_up[l], gla_gate_bias[l], ret_norm_w[l],
                  ret_norm_b[l], gla_norm_w[l], w_out[l], w_router[l], router_bias[l],
                  w_exp_gate[l], w_exp_up[l], w_exp_down[l], w_sh_gate[l], w_sh_up[l],
                  w_sh_down[l])
        zr = jnp.zeros((bp, RET_HEADS, RET_DK, RET_DV), jnp.float32)
        zg = jnp.zeros((bp, GLA_HEADS, GLA_DK, GLA_DV), jnp.float32)
        xp, rp, gp = _layer(xp, c_prompt, pos_p, zr, zg, *params)
        xs, rs, gs = _layer(xs, c_sample, pos_s, state_ret[l], state_gla[l], *params)
        ret_p.append(rp.astype(state_ret.dtype))
        gla_p.append(gp.astype(state_gla.dtype))
        ret_s.append(rs.astype(state_ret.dtype))
        gla_s.append(gs.astype(state_gla.dtype))
    return (xp, xs, jnp.stack(ret_p), jnp.stack(gla_p), jnp.stack(ret_s), jnp.stack(gla_s))
```

```python
import functools
import math

import jax
import jax.numpy as jnp
from jax import lax
from jax.experimental import pallas as pl
from jax.experimental.pallas import tpu as pltpu

F32 = jnp.float32
BF16 = jnp.bfloat16

PAST_LEN = 16384
ROPE_BASE = 10000.0
GLA_GATE_NORM = 16.0
TOP_K = 8
N_GROUPS = 8
TOPK_GROUPS = 4
ROUTED_SCALE = 2.5
EPS = 1e-6

LANES = 128
SUBLANES = 8
VMEM_LIMIT_BYTES = 56 * 1024 * 1024

GLA_CHUNK = 64
PROMPT_TILE = 256
SAMPLE_TILE = 128
ROUTER_TILE = 256
MOE_PARTS = 4
MOE_BLOCK = 128


def _dot(a, b):
    return jnp.dot(a, b, preferred_element_type=F32)


def _dot_nt(a, b):
    return lax.dot_general(a, b, (((1,), (1,)), ((), ())), preferred_element_type=F32)


def _dot_tn(a, b):
    return lax.dot_general(a, b, (((0,), (0,)), ((), ())), preferred_element_type=F32)


def _split2(x):
    hi = x.astype(BF16)
    lo = (x - hi.astype(F32)).astype(BF16)
    return hi, lo


def _split3(x):
    hi = x.astype(BF16)
    r = x - hi.astype(F32)
    mid = r.astype(BF16)
    lo = (r - mid.astype(F32)).astype(BF16)
    return hi, mid, lo


def _rms(x, g):
    return x * lax.rsqrt(jnp.mean(x * x, axis=-1, keepdims=True) + EPS) * g


def _silu(x):
    return x * jax.nn.sigmoid(x)


def _log_sigmoid(x):
    return jnp.minimum(x, 0.0) - jnp.log(1.0 + jnp.exp(-jnp.abs(x)))


def _ada_kernel(c_ref, w_ref, b_ref, o_ref):
    a_hi, a_lo = _split2(_silu(c_ref[...]))
    w_hi, w_lo = _split2(w_ref[...])
    o_ref[...] = _dot(a_hi, w_hi) + _dot(a_hi, w_lo) + _dot(a_lo, w_hi) + b_ref[...]


def _ada(c, w, b):
    rows, d = c.shape
    n = w.shape[1]
    tn = 1024
    return pl.pallas_call(
        _ada_kernel,
        grid=(n // tn,),
        in_specs=[
            pl.BlockSpec((rows, d), lambda j: (0, 0)),
            pl.BlockSpec((d, tn), lambda j: (0, j)),
            pl.BlockSpec((1, tn), lambda j: (0, j)),
        ],
        out_specs=pl.BlockSpec((rows, tn), lambda j: (0, j)),
        out_shape=jax.ShapeDtypeStruct((rows, n), F32),
        compiler_params=pltpu.CompilerParams(
            dimension_semantics=("arbitrary",), vmem_limit_bytes=VMEM_LIMIT_BYTES),
        name="ada_mod",
    )(c, w, b.reshape(1, n))


def _rotary(x, cos, sin_signed):
    return x * cos + pltpu.roll(x, x.shape[-1] // 2, axis=1) * sin_signed


def _level_reference(bc, s):
    rows, width = bc.shape
    pieces = []
    sub = lax.broadcasted_iota(jnp.int32, (SUBLANES, width), 0)
    for g in range(rows // SUBLANES):
        base = g * SUBLANES
        bounds = sorted({((base + r) // (2 * s)) * (2 * s) + s - 1 for r in range(SUBLANES)})
        piece = jnp.broadcast_to(bc[bounds[-1]:bounds[-1] + 1, :], (SUBLANES, width))
        for bm in reversed(bounds[:-1]):
            last_row_of_pair = bm + s - base
            piece = jnp.where(sub <= last_row_of_pair,
                              jnp.broadcast_to(bc[bm:bm + 1, :], (SUBLANES, width)), piece)
        pieces.append(piece)
    return jnp.concatenate(pieces, axis=0)


def _gla_level_masks(rows, heads, levels):
    i = lax.broadcasted_iota(jnp.int32, (rows, heads * rows), 0)
    j = lax.broadcasted_iota(jnp.int32, (rows, heads * rows), 1) & (rows - 1)
    masks = {}
    for s in levels:
        if s == 0:
            masks[s] = i == j
        else:
            sh = int(math.log2(s))
            masks[s] = ((i >> sh) == (j >> sh) + 1) & ((i >> (sh + 1)) == (j >> (sh + 1)))
    return masks


def _gla_intra_scores(q, k, bc, levels, masks, head_masks):
    scores = None
    for s in levels:
        if s == 0:
            qt, kt = q, k
        else:
            ref = _level_reference(bc, s)
            qt = q * jnp.exp(jnp.minimum(bc - ref, 0.0))
            kt = k * jnp.exp(jnp.minimum(ref - bc, 0.0))
        rhs_t = _block_diag_rows(kt.astype(BF16), head_masks)
        sc = jnp.where(masks[s], _dot_nt(qt.astype(BF16), rhs_t), 0.0)
        scores = sc if scores is None else scores + sc
    return scores


def _block_diag_rows(x, col_masks):
    return jnp.concatenate([x * m for m in col_masks], axis=0)


def _head_masks(rows, heads, width):
    lane = lax.broadcasted_iota(jnp.int32, (rows, heads * width), 1)
    sh = int(math.log2(width))
    return [jnp.where((lane >> sh) == h, 1.0, 0.0).astype(BF16) for h in range(heads)]


def _mixer_tail(x, mix_bf16, w_out, g1, npost, npre2, sc2, sh2, wr_hi, wr_lo):
    mix = _dot(mix_bf16, w_out)
    x1 = x + g1 * _rms(mix, npost)
    h2 = _rms(x1, npre2) * (1.0 + sc2) + sh2
    h_hi, h_lo = _split2(h2)
    logits_t = _dot_nt(wr_hi, h_hi) + _dot_nt(wr_hi, h_lo) + _dot_nt(wr_lo, h_hi)
    return x1, h2, logits_t


def _prompt_mixer_kernel(x_ref, mod_ref, npre_ref, npost_ref, npre2_ref, gbias_ref, rnw_ref, rnb_ref,
                         gnw_ref, win_ref, gup_ref, wout_ref, wrhi_ref, wrlo_ref, cos_ref, sin_ref,
                         x1_ref, h2_ref, lg_ref, sret_ref, sgla_ref,
                         proj_scr, mix_scr, dmat_scr, sret_scr, sgla_scr, *, ret_heads, gla_heads):
    t = pl.program_id(1)
    tt = x_ref.shape[1]
    rdk = 128
    rdv = 128
    gdk = gup_ref.shape[1] // gla_heads
    gdv = 128
    rq0, rk0, rv0, rg0 = 0, ret_heads * rdk, 2 * ret_heads * rdk, 2 * ret_heads * rdk + ret_heads * rdv
    gq0 = rg0 + ret_heads * rdv
    gk0 = gq0 + gla_heads * gdk
    gv0 = gk0 + gla_heads * gdk
    gg0 = gv0 + gla_heads * gdv
    ga0 = gg0 + gla_heads * gdv
    log_gamma = [math.log(1.0 - 2.0 ** (-5.0 - h)) for h in range(ret_heads)]

    @pl.when(jnp.logical_and(pl.program_id(0) == 0, t == 0))
    def _():
        i = lax.broadcasted_iota(jnp.int32, (tt, tt), 0)
        j = lax.broadcasted_iota(jnp.int32, (tt, tt), 1)
        for h in range(ret_heads):
            dmat_scr[h] = jnp.where(i >= j, jnp.exp(jnp.where(i >= j, (i - j).astype(F32) * log_gamma[h], 0.0)), 0.0)

    @pl.when(t == 0)
    def _():
        sret_scr[...] = jnp.zeros_like(sret_scr)
        sgla_scr[...] = jnp.zeros_like(sgla_scr)

    x = x_ref[0]
    mod = mod_ref[0]
    sh1, sc1, g1, sh2, sc2 = mod[0:1], mod[1:2], mod[2:3], mod[3:4], mod[4:5]
    h = _rms(x, npre_ref[...]) * (1.0 + sc1) + sh1
    proj_scr[...] = _dot(h.astype(BF16), win_ref[...])

    cos = cos_ref[...]
    sin = sin_ref[...]
    row = lax.broadcasted_iota(jnp.int32, (tt, rdk), 0).astype(F32)
    for hd in range(ret_heads):
        lg = log_gamma[hd]
        q = _rotary(proj_scr[:, rq0 + hd * rdk: rq0 + (hd + 1) * rdk], cos, sin)
        k = _rotary(proj_scr[:, rk0 + hd * rdk: rk0 + (hd + 1) * rdk], cos, sin) * (rdk ** -0.5)
        v = proj_scr[:, rv0 + hd * rdv: rv0 + (hd + 1) * rdv].astype(BF16)
        g = proj_scr[:, rg0 + hd * rdv: rg0 + (hd + 1) * rdv]
        qb = q.astype(BF16)
        s_old = sret_scr[hd]
        scores = (_dot_nt(qb, k.astype(BF16)) * dmat_scr[hd]).astype(BF16)
        o = _dot(scores, v) + jnp.exp((row + 1.0) * lg) * _dot(qb, s_old.astype(BF16))
        k_dec = (k * jnp.exp((tt - 1.0 - row) * lg)).astype(BF16)
        sret_scr[hd] = s_old * math.exp(tt * lg) + _dot_tn(k_dec, v)
        mu = jnp.mean(o, axis=-1, keepdims=True)
        oc = o - mu
        var = jnp.mean(oc * oc, axis=-1, keepdims=True)
        y = oc * lax.rsqrt(var + EPS) * rnw_ref[:, hd * rdv:(hd + 1) * rdv] + rnb_ref[:, hd * rdv:(hd + 1) * rdv]
        mix_scr[:, hd * rdv:(hd + 1) * rdv] = (_silu(g) * y).astype(BF16)

    c = GLA_CHUNK
    n_chunks = tt // c
    gw = gla_heads * gdk
    ga = proj_scr[:, ga0:ga0 + LANES].astype(BF16)
    logit = _dot(ga, gup_ref[...]) + gbias_ref[...]
    la = _log_sigmoid(logit) * (1.0 / GLA_GATE_NORM)
    ri = lax.broadcasted_iota(jnp.int32, (tt, tt), 0)
    ci = lax.broadcasted_iota(jnp.int32, (tt, tt), 1)
    csh = int(math.log2(c))
    tril = jnp.where((ri >= ci) & ((ri >> csh) == (ci >> csh)), 1.0, 0.0).astype(BF16)
    la_hi, la_mid, la_lo = _split3(la)
    bcum = _dot(tril, la_hi) + _dot(tril, la_mid) + _dot(tril, la_lo)
    levels = [s for s in (32, 16, 8, 4, 2, 1, 0) if s < c]
    masks = _gla_level_masks(c, gla_heads, levels)
    hm_k = _head_masks(c, gla_heads, gdk)
    hm_v = _head_masks(c, gla_heads, gdv)
    bd_mask = (lax.broadcasted_iota(jnp.int32, (gla_heads * gdv, gw), 0) >> int(math.log2(gdv))) == (
        lax.broadcasted_iota(jnp.int32, (gla_heads * gdv, gw), 1) >> int(math.log2(gdk)))
    for ch in range(n_chunks):
        r0 = ch * c
        q = proj_scr[r0:r0 + c, gq0:gq0 + gw] * (gdk ** -0.5)
        k = proj_scr[r0:r0 + c, gk0:gk0 + gw]
        v = proj_scr[r0:r0 + c, gv0:gv0 + gla_heads * gdv].astype(BF16)
        bc = bcum[r0:r0 + c]
        scores = _gla_intra_scores(q, k, bc, levels, masks, hm_k)
        o = _dot(scores.astype(BF16), _block_diag_rows(v, hm_v))
        st = sgla_scr[...]
        o = o + _dot_nt((q * jnp.exp(bc)).astype(BF16), st.astype(BF16))
        b_last = bc[c - 1:c, :]
        k_dec = (k * jnp.exp(b_last - bc)).astype(BF16)
        sgla_scr[...] = st * jnp.exp(b_last) + jnp.where(bd_mask, _dot_tn(v, k_dec), 0.0)
        gg = proj_scr[r0:r0 + c, gg0:gg0 + gla_heads * gdv]
        for hd in range(gla_heads):
            oh = o[:, hd * gdv:(hd + 1) * gdv]
            y = oh * lax.rsqrt(jnp.mean(oh * oh, axis=-1, keepdims=True) + EPS) * gnw_ref[:, hd * gdv:(hd + 1) * gdv]
            mix_scr[r0:r0 + c, ret_heads * rdv + hd * gdv: ret_heads * rdv + (hd + 1) * gdv] = (
                _silu(gg[:, hd * gdv:(hd + 1) * gdv]) * y).astype(BF16)

    x1, h2, logits_t = _mixer_tail(x, mix_scr[...], wout_ref[...], g1, npost_ref[...], npre2_ref[...],
                                   sc2, sh2, wrhi_ref[...], wrlo_ref[...])
    x1_ref[...] = x1
    h2_ref[...] = h2
    lg_ref[...] = logits_t

    @pl.when(t == pl.num_programs(1) - 1)
    def _():
        sret_ref[0] = sret_scr[...]
        sgla_ref[0] = sgla_scr[...]


def _prompt_mixer(x, mod, p, cos, sin, *, tile, n_total):
    b, t, d = x.shape
    nt = t // tile
    rh, gh = p["ret_heads"], p["gla_heads"]
    in_w = p["w_in"].shape[1]
    ne = p["wr_hi"].shape[0]
    gw = p["gup"].shape[1]
    full = lambda a: pl.BlockSpec(a.shape, lambda i, j: (0,) * a.ndim)
    vecs = [p["npre"], p["npost"], p["npre2"], p["gbias"], p["rnw"], p["rnb"], p["gnw"]]
    mats = [p["w_in"], p["gup"], p["w_out"], p["wr_hi"], p["wr_lo"]]
    kern = functools.partial(_prompt_mixer_kernel, ret_heads=rh, gla_heads=gh)
    return pl.pallas_call(
        kern,
        grid=(b, nt),
        in_specs=[pl.BlockSpec((1, tile, d), lambda i, j: (i, j, 0)),
                  pl.BlockSpec((1, 6, d), lambda i, j: (i, 0, 0))]
                 + [full(a) for a in vecs] + [full(a) for a in mats]
                 + [pl.BlockSpec((tile, LANES), lambda i, j: (j, 0)),
                    pl.BlockSpec((tile, LANES), lambda i, j: (j, 0))],
        out_specs=[pl.BlockSpec((tile, d), lambda i, j: (i * nt + j, 0)),
                   pl.BlockSpec((tile, d), lambda i, j: (i * nt + j, 0)),
                   pl.BlockSpec((ne, tile), lambda i, j: (0, i * nt + j)),
                   pl.BlockSpec((1, rh, 128, 128), lambda i, j: (i, 0, 0, 0)),
                   pl.BlockSpec((1, gh * 128, gw), lambda i, j: (i, 0, 0))],
        out_shape=[jax.ShapeDtypeStruct((n_total, d), F32),
                   jax.ShapeDtypeStruct((n_total, d), F32),
                   jax.ShapeDtypeStruct((ne, n_total), F32),
                   jax.ShapeDtypeStruct((b, rh, 128, 128), F32),
                   jax.ShapeDtypeStruct((b, gh * 128, gw), F32)],
        scratch_shapes=[pltpu.VMEM((tile, in_w), F32),
                        pltpu.VMEM((tile, p["w_out"].shape[0]), BF16),
                        pltpu.VMEM((rh, tile, tile), F32),
                        pltpu.VMEM((rh, 128, 128), F32),
                        pltpu.VMEM((gh * 128, gw), F32)],
        compiler_params=pltpu.CompilerParams(
            dimension_semantics=("arbitrary", "arbitrary"), vmem_limit_bytes=VMEM_LIMIT_BYTES),
        name="prompt_mixer",
    )(x, mod, *vecs, *mats, cos, sin)


def _sample_mixer_kernel(x_ref, mod_ref, npre_ref, npost_ref, npre2_ref, gbias_ref, rnw_ref, rnb_ref,
                         gnw_ref, win_ref, gup_ref, wout_ref, wrhi_ref, wrlo_ref, cos_ref, sin_ref,
                         sret_in_ref, sgla_in_ref, x1_any, h2_any, lg_any,
                         x1_ref, h2_ref, lg_ref, sret_ref, sgla_ref,
                         proj_scr, mix_scr, qrot_scr, krot_scr, oret_scr, bc_scr, gqe_scr, ogla_scr,
                         *, ret_heads, gla_heads, ts):
    del x1_any, h2_any, lg_any
    rows = x_ref.shape[0]
    nb = rows // ts
    rdk = 128
    rdv = 128
    gdk = gup_ref.shape[1] // gla_heads
    gdv = 128
    gw = gla_heads * gdk
    gvw = gla_heads * gdv
    rq0, rk0, rv0, rg0 = 0, ret_heads * rdk, 2 * ret_heads * rdk, 2 * ret_heads * rdk + ret_heads * rdv
    gq0 = rg0 + ret_heads * rdv
    gk0 = gq0 + gw
    gv0 = gk0 + gw
    gg0 = gv0 + gvw
    ga0 = gg0 + gvw
    log_gamma = [math.log(1.0 - 2.0 ** (-5.0 - h)) for h in range(ret_heads)]
    tsh = int(math.log2(ts))

    def per_row(m):
        return jnp.broadcast_to(m[:, None, :], (nb, ts, m.shape[-1])).reshape(rows, m.shape[-1])

    x = x_ref[...]
    sh1, sc1, g1, sh2, sc2 = (per_row(mod_ref[i]) for i in range(5))
    h = _rms(x, npre_ref[...]) * (1.0 + sc1) + sh1
    proj_scr[...] = _dot(h.astype(BF16), win_ref[...])

    cos = cos_ref[...]
    sin = sin_ref[...]
    ri = lax.broadcasted_iota(jnp.int32, (rows, rows), 0)
    ci = lax.broadcasted_iota(jnp.int32, (rows, rows), 1)
    same = (ri >= ci) & ((ri >> tsh) == (ci >> tsh))
    for hd in range(ret_heads):
        lg = log_gamma[hd]
        q = _rotary(proj_scr[:, rq0 + hd * rdk: rq0 + (hd + 1) * rdk], cos, sin)
        k = _rotary(proj_scr[:, rk0 + hd * rdk: rk0 + (hd + 1) * rdk], cos, sin) * (rdk ** -0.5)
        v = proj_scr[:, rv0 + hd * rdv: rv0 + (hd + 1) * rdv].astype(BF16)
        qrot_scr[:, hd * rdk:(hd + 1) * rdk] = q
        krot_scr[:, hd * rdk:(hd + 1) * rdk] = k
        dmat = jnp.where(same, jnp.exp(jnp.where(same, (ri - ci).astype(F32) * lg, 0.0)), 0.0)
        scores = (_dot_nt(q.astype(BF16), k.astype(BF16)) * dmat).astype(BF16)
        oret_scr[:, hd * rdv:(hd + 1) * rdv] = _dot(scores, v)

    ga = proj_scr[:, ga0:ga0 + LANES].astype(BF16)
    logit = _dot(ga, gup_ref[...]) + gbias_ref[...]
    la = _log_sigmoid(logit) * (1.0 / GLA_GATE_NORM)
    tril = jnp.where(same, 1.0, 0.0).astype(BF16)
    la_hi, la_mid, la_lo = _split3(la)
    bcum = _dot(tril, la_hi) + _dot(tril, la_mid) + _dot(tril, la_lo)
    bc_scr[...] = bcum
    c = min(GLA_CHUNK, rows)
    levels = [s for s in (32, 16, 8, 4, 2, 1, 0) if s < ts]
    masks = _gla_level_masks(c, gla_heads, levels)
    hm_k = _head_masks(c, gla_heads, gdk)
    hm_v = _head_masks(c, gla_heads, gdv)
    for ch in range(rows // c):
        r0 = ch * c
        q = proj_scr[r0:r0 + c, gq0:gq0 + gw] * (gdk ** -0.5)
        k = proj_scr[r0:r0 + c, gk0:gk0 + gw]
        v = proj_scr[r0:r0 + c, gv0:gv0 + gvw].astype(BF16)
        bc = bcum[r0:r0 + c]
        scores = _gla_intra_scores(q, k, bc, levels, masks, hm_k)
        ogla_scr[r0:r0 + c, :] = _dot(scores.astype(BF16), _block_diag_rows(v, hm_v))
        gqe_scr[r0:r0 + c, :] = q * jnp.exp(bc)

    trow = lax.broadcasted_iota(jnp.int32, (ts, rdk), 0).astype(F32)
    eye = lax.broadcasted_iota(jnp.int32, (gw, gw), 0) == lax.broadcasted_iota(jnp.int32, (gw, gw), 1)
    zero_blk = jnp.zeros((gdk, gdv), BF16)

    def element(b, carry):
        rs = pl.ds(pl.multiple_of(b * ts, ts), ts)
        for hd in range(ret_heads):
            lg = log_gamma[hd]
            q = qrot_scr[rs, hd * rdk:(hd + 1) * rdk]
            k = krot_scr[rs, hd * rdk:(hd + 1) * rdk]
            v = proj_scr[rs, rv0 + hd * rdv: rv0 + (hd + 1) * rdv]
            s_old = sret_in_ref[b, hd]
            oret_scr[rs, hd * rdv:(hd + 1) * rdv] += jnp.exp((trow + 1.0) * lg) * _dot(
                q.astype(BF16), s_old.astype(BF16))
            k_dec = (k * jnp.exp((ts - 1.0 - trow) * lg)).astype(BF16)
            sret_ref[b, hd] = s_old * math.exp(ts * lg) + _dot_tn(k_dec, v.astype(BF16))
        s_b = sgla_in_ref[b]
        s_bd = jnp.concatenate(
            [jnp.concatenate([s_b[hd].astype(BF16) if h2 == hd else zero_blk for h2 in range(gla_heads)], axis=1)
             for hd in range(gla_heads)], axis=0)
        ogla_scr[rs, :] += _dot(gqe_scr[rs, :].astype(BF16), s_bd)
        bc = bc_scr[rs, :]
        b_last = bc[ts - 1:ts, :]
        k_dec = (proj_scr[rs, gk0:gk0 + gw] * jnp.exp(b_last - bc)).astype(BF16)
        upd = _dot_tn(k_dec, proj_scr[rs, gv0:gv0 + gvw].astype(BF16))
        decay_col = jnp.sum(jnp.where(eye, jnp.broadcast_to(jnp.exp(b_last), (gw, gw)), 0.0), axis=1, keepdims=True)
        for hd in range(gla_heads):
            sgla_ref[b, hd] = (s_b[hd] * decay_col[hd * gdk:(hd + 1) * gdk]
                               + upd[hd * gdk:(hd + 1) * gdk, hd * gdv:(hd + 1) * gdv])
        return carry

    lax.fori_loop(0, nb, element, 0)

    for hd in range(ret_heads):
        o = oret_scr[:, hd * rdv:(hd + 1) * rdv]
        g = proj_scr[:, rg0 + hd * rdv: rg0 + (hd + 1) * rdv]
        mu = jnp.mean(o, axis=-1, keepdims=True)
        oc = o - mu
        var = jnp.mean(oc * oc, axis=-1, keepdims=True)
        y = oc * lax.rsqrt(var + EPS) * rnw_ref[:, hd * rdv:(hd + 1) * rdv] + rnb_ref[:, hd * rdv:(hd + 1) * rdv]
        mix_scr[:, hd * rdv:(hd + 1) * rdv] = (_silu(g) * y).astype(BF16)
    for hd in range(gla_heads):
        oh = ogla_scr[:, hd * gdv:(hd + 1) * gdv]
        gg = proj_scr[:, gg0 + hd * gdv: gg0 + (hd + 1) * gdv]
        y = oh * lax.rsqrt(jnp.mean(oh * oh, axis=-1, keepdims=True) + EPS) * gnw_ref[:, hd * gdv:(hd + 1) * gdv]
        mix_scr[:, ret_heads * rdv + hd * gdv: ret_heads * rdv + (hd + 1) * gdv] = (_silu(gg) * y).astype(BF16)

    x1, h2, logits_t = _mixer_tail(x, mix_scr[...], wout_ref[...], g1, npost_ref[...], npre2_ref[...],
                                   sc2, sh2, wrhi_ref[...], wrlo_ref[...])
    x1_ref[...] = x1
    h2_ref[...] = h2
    lg_ref[...] = logits_t


def _sample_mixer(x, mod_t, p, cos, sin, state_ret, state_gla, x1_buf, h2_buf, lg_buf, *, ts, tile):
    n, d = x.shape
    off = (x1_buf.shape[0] - n) // tile
    nb = tile // ts
    rh, gh = p["ret_heads"], p["gla_heads"]
    in_w = p["w_in"].shape[1]
    ne = p["wr_hi"].shape[0]
    gw = p["gup"].shape[1]
    gdk = gw // gh
    full = lambda a: pl.BlockSpec(a.shape, lambda i: (0,) * a.ndim)
    vecs = [p["npre"], p["npost"], p["npre2"], p["gbias"], p["rnw"], p["rnb"], p["gnw"]]
    mats = [p["w_in"], p["gup"], p["w_out"], p["wr_hi"], p["wr_lo"]]
    kern = functools.partial(_sample_mixer_kernel, ret_heads=rh, gla_heads=gh, ts=ts)
    n_in = 2 + len(vecs) + len(mats) + 4 + 3
    return pl.pallas_call(
        kern,
        grid=(n // tile,),
        in_specs=[pl.BlockSpec((tile, d), lambda i: (i, 0)),
                  pl.BlockSpec((6, nb, d), lambda i: (0, i, 0))]
                 + [full(a) for a in vecs] + [full(a) for a in mats]
                 + [full(cos), full(sin),
                    pl.BlockSpec((nb, rh, 128, 128), lambda i: (i, 0, 0, 0)),
                    pl.BlockSpec((nb, gh, gdk, 128), lambda i: (i, 0, 0, 0)),
                    pl.BlockSpec(memory_space=pl.ANY), pl.BlockSpec(memory_space=pl.ANY),
                    pl.BlockSpec(memory_space=pl.ANY)],
        out_specs=[pl.BlockSpec((tile, d), lambda i: (off + i, 0)),
                   pl.BlockSpec((tile, d), lambda i: (off + i, 0)),
                   pl.BlockSpec((ne, tile), lambda i: (0, off + i)),
                   pl.BlockSpec((nb, rh, 128, 128), lambda i: (i, 0, 0, 0)),
                   pl.BlockSpec((nb, gh, gdk, 128), lambda i: (i, 0, 0, 0))],
        out_shape=[jax.ShapeDtypeStruct(x1_buf.shape, F32),
                   jax.ShapeDtypeStruct(h2_buf.shape, F32),
                   jax.ShapeDtypeStruct(lg_buf.shape, F32),
                   jax.ShapeDtypeStruct(state_ret.shape, F32),
                   jax.ShapeDtypeStruct(state_gla.shape, F32)],
        input_output_aliases={n_in - 3: 0, n_in - 2: 1, n_in - 1: 2},
        scratch_shapes=[pltpu.VMEM((tile, in_w), F32),
                        pltpu.VMEM((tile, p["w_out"].shape[0]), BF16),
                        pltpu.VMEM((tile, rh * 128), F32),
                        pltpu.VMEM((tile, rh * 128), F32),
                        pltpu.VMEM((tile, rh * 128), F32),
                        pltpu.VMEM((tile, gw), F32),
                        pltpu.VMEM((tile, gw), F32),
                        pltpu.VMEM((tile, gh * 128), F32)],
        compiler_params=pltpu.CompilerParams(
            dimension_semantics=("arbitrary",), vmem_limit_bytes=VMEM_LIMIT_BYTES),
        name="sample_mixer",
    )(x, mod_t, *vecs, *mats, cos, sin, state_ret, state_gla, x1_buf, h2_buf, lg_buf)


def _router_kernel(lg_ref, bias_ref, idx_ref, w_ref, cnt_ref, cnt_scr, *, n_groups, topk_groups, top_k):
    ne, tn = lg_ref.shape
    gsz = ne // n_groups
    neg = -jnp.inf

    @pl.when(pl.program_id(1) == 0)
    def _():
        cnt_scr[...] = jnp.zeros_like(cnt_scr)

    scores = jax.nn.sigmoid(lg_ref[...])
    sel = scores + bias_ref[...][:, 0:1]
    sel3 = sel.reshape(n_groups, gsz, tn)
    mem = lax.broadcasted_iota(jnp.int32, (n_groups, gsz, tn), 1)
    m1 = jnp.max(sel3, axis=1, keepdims=True)
    first = jnp.min(jnp.where(sel3 == m1, mem, gsz), axis=1, keepdims=True)
    m2 = jnp.max(jnp.where(mem == first, neg, sel3), axis=1, keepdims=True)
    gscore = (m1 + m2).reshape(n_groups, tn)
    gi = lax.broadcasted_iota(jnp.int32, (n_groups, tn), 0)
    gsel = jnp.zeros((n_groups, tn), jnp.bool_)
    work = gscore
    for _ in range(topk_groups):
        mx = jnp.max(work, axis=0, keepdims=True)
        pick = gi == jnp.min(jnp.where(work == mx, gi, n_groups), axis=0, keepdims=True)
        gsel = jnp.logical_or(gsel, pick)
        work = jnp.where(pick, neg, work)
    emask = jnp.broadcast_to(gsel[:, None, :], (n_groups, gsz, tn)).reshape(ne, tn)
    ei = lax.broadcasted_iota(jnp.int32, (ne, tn), 0)
    work = jnp.where(emask, sel, neg)
    chosen_any = jnp.zeros((ne, tn), jnp.bool_)
    idx_rows, w_rows = [], []
    for _ in range(top_k):
        mx = jnp.max(work, axis=0, keepdims=True)
        first_e = jnp.min(jnp.where(work == mx, ei, ne), axis=0, keepdims=True)
        pick = ei == first_e
        chosen_any = jnp.logical_or(chosen_any, pick)
        work = jnp.where(pick, neg, work)
        idx_rows.append(first_e)
        w_rows.append(jnp.sum(jnp.where(pick, scores, 0.0), axis=0, keepdims=True))
    wsum = w_rows[0]
    for r in w_rows[1:]:
        wsum = wsum + r
    idx_ref[...] = jnp.concatenate(idx_rows, axis=0)
    w_ref[...] = jnp.concatenate(w_rows, axis=0) / wsum * ROUTED_SCALE
    cnt_scr[...] += _dot(jnp.where(chosen_any, 1.0, 0.0).astype(BF16), jnp.ones((tn, LANES), BF16))
    cnt_ref[0] = cnt_scr[...]


def _router(logits_t, bias, *, parts, tile):
    ne, n = logits_t.shape
    tiles = n // parts // tile
    kern = functools.partial(_router_kernel, n_groups=N_GROUPS, topk_groups=TOPK_GROUPS, top_k=TOP_K)
    return pl.pallas_call(
        kern,
        grid=(parts, tiles),
        in_specs=[pl.BlockSpec((ne, tile), lambda p, j: (0, p * tiles + j)),
                  pl.BlockSpec((ne, LANES), lambda p, j: (0, 0))],
        out_specs=[pl.BlockSpec((TOP_K, tile), lambda p, j: (0, p * tiles + j)),
                   pl.BlockSpec((TOP_K, tile), lambda p, j: (0, p * tiles + j)),
                   pl.BlockSpec((1, ne, LANES), lambda p, j: (p, 0, 0))],
        out_shape=[jax.ShapeDtypeStruct((TOP_K, n), jnp.int32),
                   jax.ShapeDtypeStruct((TOP_K, n), F32),
                   jax.ShapeDtypeStruct((parts, ne, LANES), F32)],
        scratch_shapes=[pltpu.VMEM((ne, LANES), F32)],
        compiler_params=pltpu.CompilerParams(
            dimension_semantics=("arbitrary", "arbitrary"), vmem_limit_bytes=VMEM_LIMIT_BYTES),
        name="router_topk",
    )(logits_t, jnp.broadcast_to(bias.reshape(ne, 1), (ne, LANES)))


def _dispatch_plan(idx_t, w_t, counts, *, parts, blk):
    k, n = idx_t.shape
    npart = n // parts
    ne = counts.shape[1]
    stride = npart + blk
    big = ne * stride
    local = jnp.arange(n, dtype=jnp.int32) % npart
    to_parts = lambda a: a.reshape(k, parts, npart).transpose(1, 0, 2).reshape(parts, k * npart)
    keys = to_parts(idx_t * stride + local[None, :])
    wts = to_parts(w_t)
    need = (-counts) % blk
    j = jnp.arange(blk - 1, dtype=jnp.int32)
    pad_keys = jnp.where(j[None, None, :] < need[:, :, None],
                         jnp.arange(ne, dtype=jnp.int32)[None, :, None] * stride + npart + j[None, None, :], big)
    total = k * npart + ne * (blk - 1)
    nblk = -(-total // blk)
    fill = nblk * blk - total
    all_keys = jnp.concatenate([keys, pad_keys.reshape(parts, -1), jnp.full((parts, fill), big, jnp.int32)], axis=1)
    all_w = jnp.concatenate([wts, jnp.zeros((parts, nblk * blk - k * npart), F32)], axis=1)
    sk, sw = lax.sort((all_keys, all_w), dimension=1, num_keys=1)
    valid = sk < big
    slot_tok = sk % stride
    real = jnp.logical_and(valid, slot_tok < npart)
    tok = jnp.where(real, slot_tok, 0)
    w = jnp.where(real, sw, 0.0)
    blk_valid = valid[:, ::blk]
    blk_e = jnp.where(blk_valid, sk[:, ::blk] // stride, 0)
    blk_e = jnp.where(blk_valid, blk_e, jnp.max(blk_e, axis=1, keepdims=True))
    prev = jnp.concatenate([jnp.full((parts, 1), -1, jnp.int32), blk_e[:, :-1]], axis=1)
    blk_new = jnp.logical_and(blk_valid, blk_e != prev)
    return (tok.reshape(parts * nblk, 1, blk), w.reshape(parts * nblk, 1, blk),
            blk_e.astype(jnp.int32), blk_valid.astype(jnp.int32), blk_new.astype(jnp.int32), nblk)


def _moe_kernel(be_ref, bv_ref, bn_ref, tok_ref, wts_ref, h3_ref, h2_ref, x1_ref, g2_ref, npost_ref,
                wg_ref, wu_ref, wd_ref, wsg_ref, wsu_ref, wsd_ref, ya_ref, yb_ref,
                acc_scr, g_scr, y2_scr, xs_scr, wgu_scr, wdn_scr, *, nblk, blk, ts, n_epi, tiles_a):
    p = pl.program_id(0)
    b = pl.program_id(1)
    d = h2_ref.shape[1]
    ff = wg_ref.shape[2]
    nchunk = d // LANES
    bb = jnp.minimum(b, nblk - 1)

    @pl.when(b == 0)
    def _():
        acc_scr[...] = jnp.zeros_like(acc_scr)

    @pl.when(jnp.logical_and(b < nblk, bv_ref[p, bb] == 1))
    def _():
        @pl.when(bn_ref[p, bb] == 1)
        def _():
            wgu_scr[:, :ff] = wg_ref[0].astype(BF16)
            wgu_scr[:, ff:] = wu_ref[0].astype(BF16)
            wdn_scr[...] = wd_ref[0].astype(BF16)

        def gather(r, carry):
            g_scr[r] = h3_ref[tok_ref[0, 0, r]]
            return carry

        lax.fori_loop(0, blk, gather, 0, unroll=8)
        for c in range(nchunk):
            xs_scr[:, c * LANES:(c + 1) * LANES] = g_scr[:, c, :].astype(BF16)
        gu = _dot(xs_scr[...], wgu_scr[...])
        hid = (_silu(gu[:, :ff]) * gu[:, ff:]).astype(BF16)
        y = _dot(hid, wdn_scr[...])
        for c in range(nchunk):
            y2_scr[:, c, :] = y[:, c * LANES:(c + 1) * LANES]

        def scatter(r, carry):
            t = tok_ref[0, 0, r]
            acc_scr[t] = acc_scr[t] + wts_ref[0, 0, r] * y2_scr[r]
            return carry

        lax.fori_loop(0, blk, scatter, 0, unroll=8)

    @pl.when(b >= nblk)
    def _():
        rows = ya_ref.shape[0]
        t0 = pl.multiple_of((b - nblk) * rows, rows)
        routed = jnp.concatenate([acc_scr[pl.ds(t0, rows), c, :] for c in range(nchunk)], axis=1)
        hs = h2_ref[...].astype(BF16)
        hid = (_silu(_dot(hs, wsg_ref[...])) * _dot(hs, wsu_ref[...])).astype(BF16)
        f = routed + _dot(hid, wsd_ref[...])
        g2 = g2_ref[0]
        g2 = jnp.broadcast_to(g2[:, None, :], (rows // ts, ts, d)).reshape(rows, d)
        y = x1_ref[...] + g2 * _rms(f, npost_ref[...])
        tile = p * n_epi + (b - nblk)

        @pl.when(tile < tiles_a)
        def _():
            ya_ref[...] = y

        @pl.when(tile >= tiles_a)
        def _():
            yb_ref[...] = y


def _moe(h2, x1, g2_tiles, npost, plan, w_gate, w_up, w_down, wsg, wsu, wsd, *, parts, blk, ts, n_a):
    tok, wts, blk_e, blk_valid, blk_new, nblk = plan
    n, d = h2.shape
    npart = n // parts
    ne, _, ff = w_gate.shape
    epi = LANES
    n_epi = npart // epi
    tiles_a = n_a // epi
    tiles_b = (n - n_a) // epi
    h3 = h2.reshape(n, d // LANES, LANES)
    kern = functools.partial(_moe_kernel, nblk=nblk, blk=blk, ts=ts, n_epi=n_epi, tiles_a=tiles_a)
    blk_of = lambda p, b: p * nblk + jnp.minimum(b, nblk - 1)
    epi_of = lambda p, b: p * n_epi + jnp.clip(b - nblk, 0, n_epi - 1)
    exp_of = lambda p, b, be: be[p, jnp.minimum(b, nblk - 1)]
    full = lambda a: pl.BlockSpec(a.shape, lambda p, b, be, bv, bn: (0,) * a.ndim)
    grid_spec = pltpu.PrefetchScalarGridSpec(
        num_scalar_prefetch=3,
        grid=(parts, nblk + n_epi),
        in_specs=[
            pl.BlockSpec((1, 1, blk), lambda p, b, be, bv, bn: (blk_of(p, b), 0, 0), memory_space=pltpu.SMEM),
            pl.BlockSpec((1, 1, blk), lambda p, b, be, bv, bn: (blk_of(p, b), 0, 0), memory_space=pltpu.SMEM),
            pl.BlockSpec((npart, d // LANES, LANES), lambda p, b, be, bv, bn: (p, 0, 0),
                         pipeline_mode=pl.Buffered(1)),
            pl.BlockSpec((epi, d), lambda p, b, be, bv, bn: (epi_of(p, b), 0)),
            pl.BlockSpec((epi, d), lambda p, b, be, bv, bn: (epi_of(p, b), 0)),
            pl.BlockSpec((1, epi // ts, d), lambda p, b, be, bv, bn: (epi_of(p, b), 0, 0)),
            full(npost),
            pl.BlockSpec((1, d, ff), lambda p, b, be, bv, bn: (exp_of(p, b, be), 0, 0)),
            pl.BlockSpec((1, d, ff), lambda p, b, be, bv, bn: (exp_of(p, b, be), 0, 0)),
            pl.BlockSpec((1, ff, d), lambda p, b, be, bv, bn: (exp_of(p, b, be), 0, 0)),
            full(wsg), full(wsu), full(wsd),
        ],
        out_specs=[
            pl.BlockSpec((epi, d), lambda p, b, be, bv, bn: (jnp.minimum(epi_of(p, b), tiles_a - 1), 0)),
            pl.BlockSpec((epi, d), lambda p, b, be, bv, bn: (jnp.clip(epi_of(p, b) - tiles_a, 0, tiles_b - 1), 0))],
        scratch_shapes=[pltpu.VMEM((npart, d // LANES, LANES), F32),
                        pltpu.VMEM((blk, d // LANES, LANES), F32),
                        pltpu.VMEM((blk, d // LANES, LANES), F32),
                        pltpu.VMEM((blk, d), BF16),
                        pltpu.VMEM((d, 2 * ff), BF16),
                        pltpu.VMEM((ff, d), BF16)],
    )
    return pl.pallas_call(
        kern,
        grid_spec=grid_spec,
        out_shape=[jax.ShapeDtypeStruct((n_a, d), F32), jax.ShapeDtypeStruct((n - n_a, d), F32)],
        compiler_params=pltpu.CompilerParams(
            dimension_semantics=("arbitrary", "arbitrary"), vmem_limit_bytes=VMEM_LIMIT_BYTES),
        name="moe_experts",
    )(blk_e, blk_valid, blk_new, tok, wts, h3, h2, x1, g2_tiles, npost, w_gate, w_up, w_down, wsg, wsu, wsd)


def _rope_tables(pos):
    half = 64
    inv_freq = ROPE_BASE ** (-jnp.arange(half, dtype=F32) / half)
    ang = pos[:, None] * inv_freq[None, :]
    cos, sin = jnp.cos(ang), jnp.sin(ang)
    return jnp.concatenate([cos, cos], axis=1), jnp.concatenate([-sin, sin], axis=1)


def _mixer_params(norm_pre_mix, norm_post_mix, norm_pre_ffn, w_in, gla_gate_up, gla_gate_bias,
                  ret_norm_w, ret_norm_b, gla_norm_w, w_out, w_router):
    d, in_w = w_in.shape
    pad = (-in_w) % LANES
    rank = gla_gate_up.shape[0]
    wr_hi, wr_lo = _split2(w_router.T)
    return dict(
        ret_heads=ret_norm_w.shape[0], gla_heads=gla_norm_w.shape[0],
        npre=norm_pre_mix.reshape(1, d), npost=norm_post_mix.reshape(1, d), npre2=norm_pre_ffn.reshape(1, d),
        gbias=gla_gate_bias.reshape(1, -1), rnw=ret_norm_w.reshape(1, -1), rnb=ret_norm_b.reshape(1, -1),
        gnw=gla_norm_w.reshape(1, -1),
        w_in=jnp.pad(w_in, ((0, 0), (0, pad))).astype(BF16),
        gup=jnp.pad(gla_gate_up, ((0, LANES - rank), (0, 0))).astype(BF16),
        w_out=w_out.astype(BF16), wr_hi=wr_hi, wr_lo=wr_lo)


def kernel(x_prompt, x_sample, state_ret, state_gla, c_prompt, c_sample, w_ada, b_ada, norm_pre_mix, norm_post_mix, norm_pre_ffn, norm_post_ffn, w_in, gla_gate_up, gla_gate_bias, ret_norm_w, ret_norm_b, gla_norm_w, w_out, w_router, router_bias, w_exp_gate, w_exp_up, w_exp_down, w_sh_gate, w_sh_up, w_sh_down):
    bp, tp, d = x_prompt.shape
    bs, ts, _ = x_sample.shape
    l = 0
    c_all = jnp.concatenate([c_prompt, c_sample], axis=0)
    mod = _ada(c_all, w_ada[l], b_ada[l]).reshape(bp + bs, 6, d)
    p = _mixer_params(norm_pre_mix[l], norm_post_mix[l], norm_pre_ffn[l], w_in[l], gla_gate_up[l],
                      gla_gate_bias[l], ret_norm_w[l], ret_norm_b[l], gla_norm_w[l], w_out[l], w_router[l])
    n_p, n_s = bp * tp, bs * ts
    n = n_p + n_s
    cos_p, sin_p = _rope_tables(jnp.arange(tp, dtype=F32))
    x1, h2, logits_t, sret_p, sgla_p = _prompt_mixer(x_prompt, mod[:bp], p, cos_p, sin_p,
                                                     tile=PROMPT_TILE, n_total=n)
    cos_s, sin_s = _rope_tables(PAST_LEN + jnp.arange(ts, dtype=F32))
    cos_s, sin_s = jnp.tile(cos_s, (SAMPLE_TILE // ts, 1)), jnp.tile(sin_s, (SAMPLE_TILE // ts, 1))
    x1, h2, logits_t, sret_s, sgla_s = _sample_mixer(
        x_sample.reshape(n_s, d), mod[bp:].transpose(1, 0, 2), p, cos_s, sin_s, state_ret[l], state_gla[l],
        x1, h2, logits_t, ts=ts, tile=SAMPLE_TILE)
    gh, gdk = p["gla_heads"], p["gup"].shape[1] // p["gla_heads"]
    sgla_blocks = sgla_p.reshape(bp, gh, 128, gh, gdk)
    new_gla_p = jnp.stack([sgla_blocks[:, h, :, h, :] for h in range(gh)], axis=1).transpose(0, 1, 3, 2)

    idx_t, w_t, cnt = _router(logits_t, router_bias[l], parts=MOE_PARTS, tile=ROUTER_TILE)
    plan = _dispatch_plan(idx_t, w_t, cnt[:, :, 0].astype(jnp.int32), parts=MOE_PARTS, blk=MOE_BLOCK)
    per_tile = LANES // ts
    g2 = mod[:, 5, :]
    g2_tiles = jnp.concatenate(
        [jnp.broadcast_to(jnp.repeat(g2[:bp], tp // LANES, axis=0)[:, None, :], (n_p // LANES, per_tile, d)),
         g2[bp:].reshape(n_s // LANES, per_tile, d)], axis=0)
    y_p, y_s = _moe(h2, x1, g2_tiles, norm_post_ffn[l].reshape(1, d), plan,
                    w_exp_gate[l], w_exp_up[l], w_exp_down[l],
                    w_sh_gate[l].astype(BF16), w_sh_up[l].astype(BF16), w_sh_down[l].astype(BF16),
                    parts=MOE_PARTS, blk=MOE_BLOCK, ts=ts, n_a=n_p)
    return (y_p.reshape(bp, tp, d), y_s.reshape(bs, ts, d), sret_p[None], new_gla_p[None],
            sret_s[None], sgla_s[None])
```

```python
import functools
import math

import jax
import jax.numpy as jnp
from jax import lax
from jax.experimental import pallas as pl
from jax.experimental.pallas import tpu as pltpu

F32 = jnp.float32
BF16 = jnp.bfloat16

PAST_LEN = 16384
ROPE_BASE = 10000.0
GLA_GATE_NORM = 16.0
TOP_K = 8
N_GROUPS = 8
TOPK_GROUPS = 4
ROUTED_SCALE = 2.5
EPS = 1e-6

LANES = 128
SUBLANES = 8
VMEM_LIMIT_BYTES = 56 * 1024 * 1024

GLA_CHUNK = 64
PROMPT_TILE = 256
SAMPLE_TILE = 128
ROUTER_TILE = 256
MOE_PARTS = 4
MOE_BLOCK = 128


def _dot(a, b):
    return jnp.dot(a, b, preferred_element_type=F32)


def _dot_nt(a, b):
    return lax.dot_general(a, b, (((1,), (1,)), ((), ())), preferred_element_type=F32)


def _dot_tn(a, b):
    return lax.dot_general(a, b, (((0,), (0,)), ((), ())), preferred_element_type=F32)


def _split2(x):
    hi = x.astype(BF16)
    lo = (x - hi.astype(F32)).astype(BF16)
    return hi, lo


def _split3(x):
    hi = x.astype(BF16)
    r = x - hi.astype(F32)
    mid = r.astype(BF16)
    lo = (r - mid.astype(F32)).astype(BF16)
    return hi, mid, lo


def _rms(x, g):
    return x * lax.rsqrt(jnp.mean(x * x, axis=-1, keepdims=True) + EPS) * g


def _silu(x):
    return x * jax.nn.sigmoid(x)


def _log_sigmoid(x):
    return jnp.minimum(x, 0.0) - jnp.log(1.0 + jnp.exp(-jnp.abs(x)))


def _ada_kernel(c_ref, w_ref, b_ref, o_ref):
    a_hi, a_lo = _split2(_silu(c_ref[...]))
    w_hi, w_lo = _split2(w_ref[...])
    o_ref[...] = _dot(a_hi, w_hi) + _dot(a_hi, w_lo) + _dot(a_lo, w_hi) + b_ref[...]


def _ada(c, w, b):
    rows, d = c.shape
    n = w.shape[1]
    tn = 1024
    return pl.pallas_call(
        _ada_kernel,
        grid=(n // tn,),
        in_specs=[
            pl.BlockSpec((rows, d), lambda j: (0, 0)),
            pl.BlockSpec((d, tn), lambda j: (0, j)),
            pl.BlockSpec((1, tn), lambda j: (0, j)),
        ],
        out_specs=pl.BlockSpec((rows, tn), lambda j: (0, j)),
        out_shape=jax.ShapeDtypeStruct((rows, n), F32),
        compiler_params=pltpu.CompilerParams(
            dimension_semantics=("arbitrary",), vmem_limit_bytes=VMEM_LIMIT_BYTES),
        name="ada_mod",
    )(c, w, b.reshape(1, n))


def _rotary(x, cos, sin_signed):
    return x * cos + pltpu.roll(x, x.shape[-1] // 2, axis=1) * sin_signed


def _level_reference(bc, s):
    rows, width = bc.shape
    pieces = []
    sub = lax.broadcasted_iota(jnp.int32, (SUBLANES, width), 0)
    for g in range(rows // SUBLANES):
        base = g * SUBLANES
        bounds = sorted({((base + r) // (2 * s)) * (2 * s) + s - 1 for r in range(SUBLANES)})
        piece = jnp.broadcast_to(bc[bounds[-1]:bounds[-1] + 1, :], (SUBLANES, width))
        for bm in reversed(bounds[:-1]):
            last_row_of_pair = bm + s - base
            piece = jnp.where(sub <= last_row_of_pair,
                              jnp.broadcast_to(bc[bm:bm + 1, :], (SUBLANES, width)), piece)
        pieces.append(piece)
    return jnp.concatenate(pieces, axis=0)


def _gla_level_masks(rows, heads, levels):
    i = lax.broadcasted_iota(jnp.int32, (rows, heads * rows), 0)
    j = lax.broadcasted_iota(jnp.int32, (rows, heads * rows), 1) & (rows - 1)
    masks = {}
    for s in levels:
        if s == 0:
            masks[s] = i == j
        else:
            sh = int(math.log2(s))
            masks[s] = ((i >> sh) == (j >> sh) + 1) & ((i >> (sh + 1)) == (j >> (sh + 1)))
    return masks


def _gla_intra_scores(q, k, bc, levels, masks, head_masks):
    scores = None
    for s in levels:
        if s == 0:
            qt, kt = q, k
        else:
            ref = _level_reference(bc, s)
            qt = q * jnp.exp(jnp.minimum(bc - ref, 0.0))
            kt = k * jnp.exp(jnp.minimum(ref - bc, 0.0))
        rhs_t = _block_diag_rows(kt.astype(BF16), head_masks)
        sc = jnp.where(masks[s], _dot_nt(qt.astype(BF16), rhs_t), 0.0)
        scores = sc if scores is None else scores + sc
    return scores


def _block_diag_rows(x, col_masks):
    return jnp.concatenate([x * m for m in col_masks], axis=0)


def _head_masks(rows, heads, width):
    lane = lax.broadcasted_iota(jnp.int32, (rows, heads * width), 1)
    sh = int(math.log2(width))
    return [jnp.where((lane >> sh) == h, 1.0, 0.0).astype(BF16) for h in range(heads)]


def _mixer_tail(x, mix_bf16, w_out, g1, npost, npre2, sc2, sh2, wr_hi, wr_lo):
    mix = _dot(mix_bf16, w_out)
    x1 = x + g1 * _rms(mix, npost)
    h2 = _rms(x1, npre2) * (1.0 + sc2) + sh2
    h_hi, h_lo = _split2(h2)
    logits_t = _dot_nt(wr_hi, h_hi) + _dot_nt(wr_hi, h_lo) + _dot_nt(wr_lo, h_hi)
    return x1, h2, logits_t


def _prompt_mixer_kernel(x_ref, mod_ref, npre_ref, npost_ref, npre2_ref, gbias_ref, rnw_ref, rnb_ref,
                         gnw_ref, win_ref, gup_ref, wout_ref, wrhi_ref, wrlo_ref, cos_ref, sin_ref,
                         x1_ref, h2_ref, lg_ref, sret_ref, sgla_ref,
                         proj_scr, mix_scr, dmat_scr, sret_scr, sgla_scr, *, ret_heads, gla_heads):
    t = pl.program_id(1)
    tt = x_ref.shape[1]
    rdk = 128
    rdv = 128
    gdk = gup_ref.shape[1] // gla_heads
    gdv = 128
    rq0, rk0, rv0, rg0 = 0, ret_heads * rdk, 2 * ret_heads * rdk, 2 * ret_heads * rdk + ret_heads * rdv
    gq0 = rg0 + ret_heads * rdv
    gk0 = gq0 + gla_heads * gdk
    gv0 = gk0 + gla_heads * gdk
    gg0 = gv0 + gla_heads * gdv
    ga0 = gg0 + gla_heads * gdv
    log_gamma = [math.log(1.0 - 2.0 ** (-5.0 - h)) for h in range(ret_heads)]

    @pl.when(jnp.logical_and(pl.program_id(0) == 0, t == 0))
    def _():
        i = lax.broadcasted_iota(jnp.int32, (tt, tt), 0)
        j = lax.broadcasted_iota(jnp.int32, (tt, tt), 1)
        for h in range(ret_heads):
            dmat_scr[h] = jnp.where(i >= j, jnp.exp(jnp.where(i >= j, (i - j).astype(F32) * log_gamma[h], 0.0)), 0.0)

    @pl.when(t == 0)
    def _():
        sret_scr[...] = jnp.zeros_like(sret_scr)
        sgla_scr[...] = jnp.zeros_like(sgla_scr)

    x = x_ref[0]
    mod = mod_ref[0]
    sh1, sc1, g1, sh2, sc2 = mod[0:1], mod[1:2], mod[2:3], mod[3:4], mod[4:5]
    h = _rms(x, npre_ref[...]) * (1.0 + sc1) + sh1
    proj_scr[...] = _dot(h.astype(BF16), win_ref[...])

    cos = cos_ref[...]
    sin = sin_ref[...]
    row = lax.broadcasted_iota(jnp.int32, (tt, rdk), 0).astype(F32)
    for hd in range(ret_heads):
        lg = log_gamma[hd]
        q = _rotary(proj_scr[:, rq0 + hd * rdk: rq0 + (hd + 1) * rdk], cos, sin)
        k = _rotary(proj_scr[:, rk0 + hd * rdk: rk0 + (hd + 1) * rdk], cos, sin) * (rdk ** -0.5)
        v = proj_scr[:, rv0 + hd * rdv: rv0 + (hd + 1) * rdv].astype(BF16)
        g = proj_scr[:, rg0 + hd * rdv: rg0 + (hd + 1) * rdv]
        qb = q.astype(BF16)
        s_old = sret_scr[hd]
        scores = (_dot_nt(qb, k.astype(BF16)) * dmat_scr[hd]).astype(BF16)
        o = _dot(scores, v) + jnp.exp((row + 1.0) * lg) * _dot(qb, s_old.astype(BF16))
        k_dec = (k * jnp.exp((tt - 1.0 - row) * lg)).astype(BF16)
        sret_scr[hd] = s_old * math.exp(tt * lg) + _dot_tn(k_dec, v)
        mu = jnp.mean(o, axis=-1, keepdims=True)
        oc = o - mu
        var = jnp.mean(oc * oc, axis=-1, keepdims=True)
        y = oc * lax.rsqrt(var + EPS) * rnw_ref[:, hd * rdv:(hd + 1) * rdv] + rnb_ref[:, hd * rdv:(hd + 1) * rdv]
        mix_scr[:, hd * rdv:(hd + 1) * rdv] = (_silu(g) * y).astype(BF16)

    c = GLA_CHUNK
    n_chunks = tt // c
    gw = gla_heads * gdk
    ga = proj_scr[:, ga0:ga0 + LANES].astype(BF16)
    logit = _dot(ga, gup_ref[...]) + gbias_ref[...]
    la = _log_sigmoid(logit) * (1.0 / GLA_GATE_NORM)
    ri = lax.broadcasted_iota(jnp.int32, (tt, tt), 0)
    ci = lax.broadcasted_iota(jnp.int32, (tt, tt), 1)
    csh = int(math.log2(c))
    tril = jnp.where((ri >= ci) & ((ri >> csh) == (ci >> csh)), 1.0, 0.0).astype(BF16)
    la_hi, la_mid, la_lo = _split3(la)
    bcum = _dot(tril, la_hi) + _dot(tril, la_mid) + _dot(tril, la_lo)
    levels = [s for s in (32, 16, 8, 4, 2, 1, 0) if s < c]
    masks = _gla_level_masks(c, gla_heads, levels)
    hm_k = _head_masks(c, gla_heads, gdk)
    hm_v = _head_masks(c, gla_heads, gdv)
    bd_mask = (lax.broadcasted_iota(jnp.int32, (gla_heads * gdv, gw), 0) >> int(math.log2(gdv))) == (
        lax.broadcasted_iota(jnp.int32, (gla_heads * gdv, gw), 1) >> int(math.log2(gdk)))
    for ch in range(n_chunks):
        r0 = ch * c
        q = proj_scr[r0:r0 + c, gq0:gq0 + gw] * (gdk ** -0.5)
        k = proj_scr[r0:r0 + c, gk0:gk0 + gw]
        v = proj_scr[r0:r0 + c, gv0:gv0 + gla_heads * gdv].astype(BF16)
        bc = bcum[r0:r0 + c]
        scores = _gla_intra_scores(q, k, bc, levels, masks, hm_k)
        o = _dot(scores.astype(BF16), _block_diag_rows(v, hm_v))
        st = sgla_scr[...]
        o = o + _dot_nt((q * jnp.exp(bc)).astype(BF16), st.astype(BF16))
        b_last = bc[c - 1:c, :]
        k_dec = (k * jnp.exp(b_last - bc)).astype(BF16)
        sgla_scr[...] = st * jnp.exp(b_last) + jnp.where(bd_mask, _dot_tn(v, k_dec), 0.0)
        gg = proj_scr[r0:r0 + c, gg0:gg0 + gla_heads * gdv]
        for hd in range(gla_heads):
            oh = o[:, hd * gdv:(hd + 1) * gdv]
            y = oh * lax.rsqrt(jnp.mean(oh * oh, axis=-1, keepdims=True) + EPS) * gnw_ref[:, hd * gdv:(hd + 1) * gdv]
            mix_scr[r0:r0 + c, ret_heads * rdv + hd * gdv: ret_heads * rdv + (hd + 1) * gdv] = (
                _silu(gg[:, hd * gdv:(hd + 1) * gdv]) * y).astype(BF16)

    x1, h2, logits_t = _mixer_tail(x, mix_scr[...], wout_ref[...], g1, npost_ref[...], npre2_ref[...],
                                   sc2, sh2, wrhi_ref[...], wrlo_ref[...])
    x1_ref[...] = x1
    h2_ref[...] = h2
    lg_ref[...] = logits_t

    @pl.when(t == pl.num_programs(1) - 1)
    def _():
        sret_ref[0] = sret_scr[...]
        sgla_ref[0] = sgla_scr[...]


def _prompt_mixer(x, mod, p, cos, sin, *, tile, n_total):
    b, t, d = x.shape
    nt = t // tile
    rh, gh = p["ret_heads"], p["gla_heads"]
    in_w = p["w_in"].shape[1]
    ne = p["wr_hi"].shape[0]
    gw = p["gup"].shape[1]
    full = lambda a: pl.BlockSpec(a.shape, lambda i, j: (0,) * a.ndim)
    vecs = [p["npre"], p["npost"], p["npre2"], p["gbias"], p["rnw"], p["rnb"], p["gnw"]]
    mats = [p["w_in"], p["gup"], p["w_out"], p["wr_hi"], p["wr_lo"]]
    kern = functools.partial(_prompt_mixer_kernel, ret_heads=rh, gla_heads=gh)
    return pl.pallas_call(
        kern,
        grid=(b, nt),
        in_specs=[pl.BlockSpec((1, tile, d), lambda i, j: (i, j, 0)),
                  pl.BlockSpec((1, 6, d), lambda i, j: (i, 0, 0))]
                 + [full(a) for a in vecs] + [full(a) for a in mats]
                 + [pl.BlockSpec((tile, LANES), lambda i, j: (j, 0)),
                    pl.BlockSpec((tile, LANES), lambda i, j: (j, 0))],
        out_specs=[pl.BlockSpec((tile, d), lambda i, j: (i * nt + j, 0)),
                   pl.BlockSpec((tile, d), lambda i, j: (i * nt + j, 0)),
                   pl.BlockSpec((ne, tile), lambda i, j: (0, i * nt + j)),
                   pl.BlockSpec((1, rh, 128, 128), lambda i, j: (i, 0, 0, 0)),
                   pl.BlockSpec((1, gh * 128, gw), lambda i, j: (i, 0, 0))],
        out_shape=[jax.ShapeDtypeStruct((n_total, d), F32),
                   jax.ShapeDtypeStruct((n_total, d), F32),
                   jax.ShapeDtypeStruct((ne, n_total), F32),
                   jax.ShapeDtypeStruct((b, rh, 128, 128), F32),
                   jax.ShapeDtypeStruct((b, gh * 128, gw), F32)],
        scratch_shapes=[pltpu.VMEM((tile, in_w), F32),
                        pltpu.VMEM((tile, p["w_out"].shape[0]), BF16),
                        pltpu.VMEM((rh, tile, tile), F32),
                        pltpu.VMEM((rh, 128, 128), F32),
                        pltpu.VMEM((gh * 128, gw), F32)],
        compiler_params=pltpu.CompilerParams(
            dimension_semantics=("arbitrary", "arbitrary"), vmem_limit_bytes=VMEM_LIMIT_BYTES),
        name="prompt_mixer",
    )(x, mod, *vecs, *mats, cos, sin)


def _sample_mixer_kernel(x_ref, mod_ref, npre_ref, npost_ref, npre2_ref, gbias_ref, rnw_ref, rnb_ref,
                         gnw_ref, win_ref, gup_ref, wout_ref, wrhi_ref, wrlo_ref, cos_ref, sin_ref,
                         sret_in_ref, sgla_in_ref, x1_any, h2_any, lg_any,
                         x1_ref, h2_ref, lg_ref, sret_ref, sgla_ref,
                         proj_scr, mix_scr, qrot_scr, krot_scr, oret_scr, bc_scr, gqe_scr, ogla_scr,
                         *, ret_heads, gla_heads, ts):
    del x1_any, h2_any, lg_any
    rows = x_ref.shape[0]
    nb = rows // ts
    rdk = 128
    rdv = 128
    gdk = gup_ref.shape[1] // gla_heads
    gdv = 128
    gw = gla_heads * gdk
    gvw = gla_heads * gdv
    rq0, rk0, rv0, rg0 = 0, ret_heads * rdk, 2 * ret_heads * rdk, 2 * ret_heads * rdk + ret_heads * rdv
    gq0 = rg0 + ret_heads * rdv
    gk0 = gq0 + gw
    gv0 = gk0 + gw
    gg0 = gv0 + gvw
    ga0 = gg0 + gvw
    log_gamma = [math.log(1.0 - 2.0 ** (-5.0 - h)) for h in range(ret_heads)]
    tsh = int(math.log2(ts))

    def per_row(m):
        return jnp.broadcast_to(m[:, None, :], (nb, ts, m.shape[-1])).reshape(rows, m.shape[-1])

    x = x_ref[...]
    sh1, sc1, g1, sh2, sc2 = (per_row(mod_ref[i]) for i in range(5))
    h = _rms(x, npre_ref[...]) * (1.0 + sc1) + sh1
    proj_scr[...] = _dot(h.astype(BF16), win_ref[...])

    cos = cos_ref[...]
    sin = sin_ref[...]
    ri = lax.broadcasted_iota(jnp.int32, (rows, rows), 0)
    ci = lax.broadcasted_iota(jnp.int32, (rows, rows), 1)
    same = (ri >= ci) & ((ri >> tsh) == (ci >> tsh))
    for hd in range(ret_heads):
        lg = log_gamma[hd]
        q = _rotary(proj_scr[:, rq0 + hd * rdk: rq0 + (hd + 1) * rdk], cos, sin)
        k = _rotary(proj_scr[:, rk0 + hd * rdk: rk0 + (hd + 1) * rdk], cos, sin) * (rdk ** -0.5)
        v = proj_scr[:, rv0 + hd * rdv: rv0 + (hd + 1) * rdv].astype(BF16)
        qrot_scr[:, hd * rdk:(hd + 1) * rdk] = q
        krot_scr[:, hd * rdk:(hd + 1) * rdk] = k
        dmat = jnp.where(same, jnp.exp(jnp.where(same, (ri - ci).astype(F32) * lg, 0.0)), 0.0)
        scores = (_dot_nt(q.astype(BF16), k.astype(BF16)) * dmat).astype(BF16)
        oret_scr[:, hd * rdv:(hd + 1) * rdv] = _dot(scores, v)

    ga = proj_scr[:, ga0:ga0 + LANES].astype(BF16)
    logit = _dot(ga, gup_ref[...]) + gbias_ref[...]
    la = _log_sigmoid(logit) * (1.0 / GLA_GATE_NORM)
    tril = jnp.where(same, 1.0, 0.0).astype(BF16)
    la_hi, la_mid, la_lo = _split3(la)
    bcum = _dot(tril, la_hi) + _dot(tril, la_mid) + _dot(tril, la_lo)
    bc_scr[...] = bcum
    c = min(GLA_CHUNK, rows)
    levels = [s for s in (32, 16, 8, 4, 2, 1, 0) if s < ts]
    masks = _gla_level_masks(c, gla_heads, levels)
    hm_k = _head_masks(c, gla_heads, gdk)
    hm_v = _head_masks(c, gla_heads, gdv)
    for ch in range(rows // c):
        r0 = ch * c
        q = proj_scr[r0:r0 + c, gq0:gq0 + gw] * (gdk ** -0.5)
        k = proj_scr[r0:r0 + c, gk0:gk0 + gw]
        v = proj_scr[r0:r0 + c, gv0:gv0 + gvw].astype(BF16)
        bc = bcum[r0:r0 + c]
        scores = _gla_intra_scores(q, k, bc, levels, masks, hm_k)
        ogla_scr[r0:r0 + c, :] = _dot(scores.astype(BF16), _block_diag_rows(v, hm_v))
        gqe_scr[r0:r0 + c, :] = q * jnp.exp(bc)

    trow = lax.broadcasted_iota(jnp.int32, (ts, rdk), 0).astype(F32)
    eye = lax.broadcasted_iota(jnp.int32, (gw, gw), 0) == lax.broadcasted_iota(jnp.int32, (gw, gw), 1)
    zero_blk = jnp.zeros((gdk, gdv), BF16)

    def element(b, carry):
        rs = pl.ds(pl.multiple_of(b * ts, ts), ts)
        for hd in range(ret_heads):
            lg = log_gamma[hd]
            q = qrot_scr[rs, hd * rdk:(hd + 1) * rdk]
            k = krot_scr[rs, hd * rdk:(hd + 1) * rdk]
            v = proj_scr[rs, rv0 + hd * rdv: rv0 + (hd + 1) * rdv]
            s_old = sret_in_ref[b, hd]
            oret_scr[rs, hd * rdv:(hd + 1) * rdv] += jnp.exp((trow + 1.0) * lg) * _dot(
                q.astype(BF16), s_old.astype(BF16))
            k_dec = (k * jnp.exp((ts - 1.0 - trow) * lg)).astype(BF16)
            sret_ref[b, hd] = s_old * math.exp(ts * lg) + _dot_tn(k_dec, v.astype(BF16))
        s_b = sgla_in_ref[b]
        s_bd = jnp.concatenate(
            [jnp.concatenate([s_b[hd].astype(BF16) if h2 == hd else zero_blk for h2 in range(gla_heads)], axis=1)
             for hd in range(gla_heads)], axis=0)
        ogla_scr[rs, :] += _dot(gqe_scr[rs, :].astype(BF16), s_bd)
        bc = bc_scr[rs, :]
        b_last = bc[ts - 1:ts, :]
        k_dec = (proj_scr[rs, gk0:gk0 + gw] * jnp.exp(b_last - bc)).astype(BF16)
        upd = _dot_tn(k_dec, proj_scr[rs, gv0:gv0 + gvw].astype(BF16))
        decay_col = jnp.sum(jnp.where(eye, jnp.broadcast_to(jnp.exp(b_last), (gw, gw)), 0.0), axis=1, keepdims=True)
        for hd in range(gla_heads):
            sgla_ref[b, hd] = (s_b[hd] * decay_col[hd * gdk:(hd + 1) * gdk]
                               + upd[hd * gdk:(hd + 1) * gdk, hd * gdv:(hd + 1) * gdv])
        return carry

    lax.fori_loop(0, nb, element, 0)

    for hd in range(ret_heads):
        o = oret_scr[:, hd * rdv:(hd + 1) * rdv]
        g = proj_scr[:, rg0 + hd * rdv: rg0 + (hd + 1) * rdv]
        mu = jnp.mean(o, axis=-1, keepdims=True)
        oc = o - mu
        var = jnp.mean(oc * oc, axis=-1, keepdims=True)
        y = oc * lax.rsqrt(var + EPS) * rnw_ref[:, hd * rdv:(hd + 1) * rdv] + rnb_ref[:, hd * rdv:(hd + 1) * rdv]
        mix_scr[:, hd * rdv:(hd + 1) * rdv] = (_silu(g) * y).astype(BF16)
    for hd in range(gla_heads):
        oh = ogla_scr[:, hd * gdv:(hd + 1) * gdv]
        gg = proj_scr[:, gg0 + hd * gdv: gg0 + (hd + 1) * gdv]
        y = oh * lax.rsqrt(jnp.mean(oh * oh, axis=-1, keepdims=True) + EPS) * gnw_ref[:, hd * gdv:(hd + 1) * gdv]
        mix_scr[:, ret_heads * rdv + hd * gdv: ret_heads * rdv + (hd + 1) * gdv] = (_silu(gg) * y).astype(BF16)

    x1, h2, logits_t = _mixer_tail(x, mix_scr[...], wout_ref[...], g1, npost_ref[...], npre2_ref[...],
                                   sc2, sh2, wrhi_ref[...], wrlo_ref[...])
    x1_ref[...] = x1
    h2_ref[...] = h2
    lg_ref[...] = logits_t


def _sample_mixer(x, mod_t, p, cos, sin, state_ret, state_gla, x1_buf, h2_buf, lg_buf, *, ts, tile):
    n, d = x.shape
    off = (x1_buf.shape[0] - n) // tile
    nb = tile // ts
    rh, gh = p["ret_heads"], p["gla_heads"]
    in_w = p["w_in"].shape[1]
    ne = p["wr_hi"].shape[0]
    gw = p["gup"].shape[1]
    gdk = gw // gh
    full = lambda a: pl.BlockSpec(a.shape, lambda i: (0,) * a.ndim)
    vecs = [p["npre"], p["npost"], p["npre2"], p["gbias"], p["rnw"], p["rnb"], p["gnw"]]
    mats = [p["w_in"], p["gup"], p["w_out"], p["wr_hi"], p["wr_lo"]]
    kern = functools.partial(_sample_mixer_kernel, ret_heads=rh, gla_heads=gh, ts=ts)
    n_in = 2 + len(vecs) + len(mats) + 4 + 3
    return pl.pallas_call(
        kern,
        grid=(n // tile,),
        in_specs=[pl.BlockSpec((tile, d), lambda i: (i, 0)),
                  pl.BlockSpec((6, nb, d), lambda i: (0, i, 0))]
                 + [full(a) for a in vecs] + [full(a) for a in mats]
                 + [full(cos), full(sin),
                    pl.BlockSpec((nb, rh, 128, 128), lambda i: (i, 0, 0, 0)),
                    pl.BlockSpec((nb, gh, gdk, 128), lambda i: (i, 0, 0, 0)),
                    pl.BlockSpec(memory_space=pl.ANY), pl.BlockSpec(memory_space=pl.ANY),
                    pl.BlockSpec(memory_space=pl.ANY)],
        out_specs=[pl.BlockSpec((tile, d), lambda i: (off + i, 0)),
                   pl.BlockSpec((tile, d), lambda i: (off + i, 0)),
                   pl.BlockSpec((ne, tile), lambda i: (0, off + i)),
                   pl.BlockSpec((nb, rh, 128, 128), lambda i: (i, 0, 0, 0)),
                   pl.BlockSpec((nb, gh, gdk, 128), lambda i: (i, 0, 0, 0))],
        out_shape=[jax.ShapeDtypeStruct(x1_buf.shape, F32),
                   jax.ShapeDtypeStruct(h2_buf.shape, F32),
                   jax.ShapeDtypeStruct(lg_buf.shape, F32),
                   jax.ShapeDtypeStruct(state_ret.shape, F32),
                   jax.ShapeDtypeStruct(state_gla.shape, F32)],
        input_output_aliases={n_in - 3: 0, n_in - 2: 1, n_in - 1: 2},
        scratch_shapes=[pltpu.VMEM((tile, in_w), F32),
                        pltpu.VMEM((tile, p["w_out"].shape[0]), BF16),
                        pltpu.VMEM((tile, rh * 128), F32),
                        pltpu.VMEM((tile, rh * 128), F32),
                        pltpu.VMEM((tile, rh * 128), F32),
                        pltpu.VMEM((tile, gw), F32),
                        pltpu.VMEM((tile, gw), F32),
                        pltpu.VMEM((tile, gh * 128), F32)],
        compiler_params=pltpu.CompilerParams(
            dimension_semantics=("arbitrary",), vmem_limit_bytes=VMEM_LIMIT_BYTES),
        name="sample_mixer",
    )(x, mod_t, *vecs, *mats, cos, sin, state_ret, state_gla, x1_buf, h2_buf, lg_buf)


def _router_kernel(lg_ref, bias_ref, idx_ref, w_ref, cnt_ref, cnt_scr, *, n_groups, topk_groups, top_k):
    ne, tn = lg_ref.shape
    gsz = ne // n_groups
    neg = -jnp.inf

    @pl.when(pl.program_id(1) == 0)
    def _():
        cnt_scr[...] = jnp.zeros_like(cnt_scr)

    scores = jax.nn.sigmoid(lg_ref[...])
    sel = scores + bias_ref[...][:, 0:1]
    sel3 = sel.reshape(n_groups, gsz, tn)
    mem = lax.broadcasted_iota(jnp.int32, (n_groups, gsz, tn), 1)
    m1 = jnp.max(sel3, axis=1, keepdims=True)
    first = jnp.min(jnp.where(sel3 == m1, mem, gsz), axis=1, keepdims=True)
    m2 = jnp.max(jnp.where(mem == first, neg, sel3), axis=1, keepdims=True)
    gscore = (m1 + m2).reshape(n_groups, tn)
    gi = lax.broadcasted_iota(jnp.int32, (n_groups, tn), 0)
    gsel = jnp.zeros((n_groups, tn), jnp.bool_)
    work = gscore
    for _ in range(topk_groups):
        mx = jnp.max(work, axis=0, keepdims=True)
        pick = gi == jnp.min(jnp.where(work == mx, gi, n_groups), axis=0, keepdims=True)
        gsel = jnp.logical_or(gsel, pick)
        work = jnp.where(pick, neg, work)
    emask = jnp.broadcast_to(gsel[:, None, :], (n_groups, gsz, tn)).reshape(ne, tn)
    ei = lax.broadcasted_iota(jnp.int32, (ne, tn), 0)
    work = jnp.where(emask, sel, neg)
    chosen_any = jnp.zeros((ne, tn), jnp.bool_)
    idx_rows, w_rows = [], []
    for _ in range(top_k):
        mx = jnp.max(work, axis=0, keepdims=True)
        first_e = jnp.min(jnp.where(work == mx, ei, ne), axis=0, keepdims=True)
        pick = ei == first_e
        chosen_any = jnp.logical_or(chosen_any, pick)
        work = jnp.where(pick, neg, work)
        idx_rows.append(first_e)
        w_rows.append(jnp.sum(jnp.where(pick, scores, 0.0), axis=0, keepdims=True))
    wsum = w_rows[0]
    for r in w_rows[1:]:
        wsum = wsum + r
    idx_ref[...] = jnp.concatenate(idx_rows, axis=0)
    w_ref[...] = jnp.concatenate(w_rows, axis=0) / wsum * ROUTED_SCALE
    cnt_scr[...] += _dot(jnp.where(chosen_any, 1.0, 0.0).astype(BF16), jnp.ones((tn, LANES), BF16))
    cnt_ref[0] = cnt_scr[...]


def _router(logits_t, bias, *, parts, tile):
    ne, n = logits_t.shape
    tiles = n // parts // tile
    kern = functools.partial(_router_kernel, n_groups=N_GROUPS, topk_groups=TOPK_GROUPS, top_k=TOP_K)
    return pl.pallas_call(
        kern,
        grid=(parts, tiles),
        in_specs=[pl.BlockSpec((ne, tile), lambda p, j: (0, p * tiles + j)),
                  pl.BlockSpec((ne, LANES), lambda p, j: (0, 0))],
        out_specs=[pl.BlockSpec((TOP_K, tile), lambda p, j: (0, p * tiles + j)),
                   pl.BlockSpec((TOP_K, tile), lambda p, j: (0, p * tiles + j)),
                   pl.BlockSpec((1, ne, LANES), lambda p, j: (p, 0, 0))],
        out_shape=[jax.ShapeDtypeStruct((TOP_K, n), jnp.int32),
                   jax.ShapeDtypeStruct((TOP_K, n), F32),
                   jax.ShapeDtypeStruct((parts, ne, LANES), F32)],
        scratch_shapes=[pltpu.VMEM((ne, LANES), F32)],
        compiler_params=pltpu.CompilerParams(
            dimension_semantics=("arbitrary", "arbitrary"), vmem_limit_bytes=VMEM_LIMIT_BYTES),
        name="router_topk",
    )(logits_t, jnp.broadcast_to(bias.reshape(ne, 1), (ne, LANES)))


def _dispatch_plan(idx_t, w_t, counts, *, parts, blk):
    k, n = idx_t.shape
    npart = n // parts
    ne = counts.shape[1]
    stride = npart + blk
    big = ne * stride
    local = jnp.arange(n, dtype=jnp.int32) % npart
    to_parts = lambda a: a.reshape(k, parts, npart).transpose(1, 0, 2).reshape(parts, k * npart)
    keys = to_parts(idx_t * stride + local[None, :])
    wts = to_parts(w_t)
    need = (-counts) % blk
    j = jnp.arange(blk - 1, dtype=jnp.int32)
    pad_keys = jnp.where(j[None, None, :] < need[:, :, None],
                         jnp.arange(ne, dtype=jnp.int32)[None, :, None] * stride + npart + j[None, None, :], big)
    total = k * npart + ne * (blk - 1)
    nblk = -(-total // blk)
    fill = nblk * blk - total
    all_keys = jnp.concatenate([keys, pad_keys.reshape(parts, -1), jnp.full((parts, fill), big, jnp.int32)], axis=1)
    all_w = jnp.concatenate([wts, jnp.zeros((parts, nblk * blk - k * npart), F32)], axis=1)
    sk, sw = lax.sort((all_keys, all_w), dimension=1, num_keys=1)
    valid = sk < big
    slot_tok = sk % stride
    real = jnp.logical_and(valid, slot_tok < npart)
    tok = jnp.where(real, slot_tok, npart)
    w = jnp.where(real, sw, 0.0)
    blk_valid = valid[:, ::blk]
    blk_e = jnp.where(blk_valid, sk[:, ::blk] // stride, 0)
    blk_e = jnp.where(blk_valid, blk_e, jnp.max(blk_e, axis=1, keepdims=True))
    prev = jnp.concatenate([jnp.full((parts, 1), -1, jnp.int32), blk_e[:, :-1]], axis=1)
    blk_new = jnp.logical_and(blk_valid, blk_e != prev)
    return (tok.reshape(parts * nblk, 1, blk), w.reshape(parts * nblk, 1, blk),
            blk_e.astype(jnp.int32), blk_valid.astype(jnp.int32), blk_new.astype(jnp.int32), nblk)


def _moe_kernel(be_ref, bv_ref, bn_ref, tok_ref, wts_ref, h3_ref, h2_ref, x1_ref, g2_ref, npost_ref,
                wg_ref, wu_ref, wd_ref, wsg_ref, wsu_ref, wsd_ref, ya_ref, yb_ref,
                acc_scr, g_scr, y2_scr, xs_scr, wgu_scr, wdn_scr, *, nblk, blk, ts, n_epi, tiles_a):
    p = pl.program_id(0)
    b = pl.program_id(1)
    d = h2_ref.shape[1]
    ff = wg_ref.shape[2]
    nchunk = d // LANES
    last_tok = h3_ref.shape[0] - 1
    bb = jnp.minimum(b, nblk - 1)

    @pl.when(b == 0)
    def _():
        acc_scr[...] = jnp.zeros_like(acc_scr)

    @pl.when(jnp.logical_and(b < nblk, bv_ref[p, bb] == 1))
    def _():
        @pl.when(bn_ref[p, bb] == 1)
        def _():
            wgu_scr[:, :ff] = wg_ref[0].astype(BF16)
            wgu_scr[:, ff:] = wu_ref[0].astype(BF16)
            wdn_scr[...] = wd_ref[0].astype(BF16)

        tile_rows = lambda i: pl.ds(pl.multiple_of(i * nchunk, nchunk), nchunk)

        def gather(r, carry):
            g_scr[tile_rows(r), :] = h3_ref[jnp.minimum(tok_ref[0, 0, r], last_tok)]
            return carry

        lax.fori_loop(0, blk, gather, 0, unroll=8)
        for c in range(nchunk):
            xs_scr[:, c * LANES:(c + 1) * LANES] = g_scr[pl.ds(c, blk, stride=nchunk), :].astype(BF16)
        gu = _dot(xs_scr[...], wgu_scr[...])
        hid = (_silu(gu[:, :ff]) * gu[:, ff:]).astype(BF16)
        y = _dot(hid, wdn_scr[...])
        for c in range(nchunk):
            y2_scr[pl.ds(c, blk, stride=nchunk), :] = y[:, c * LANES:(c + 1) * LANES]

        def scatter(g, carry):
            rows = [g * SUBLANES + i for i in range(SUBLANES)]
            toks = [tok_ref[0, 0, r] for r in rows]
            new = [acc_scr[tile_rows(t), :] + wts_ref[0, 0, r] * y2_scr[tile_rows(r), :] for t, r in zip(toks, rows)]
            for t, v in zip(toks, new):
                acc_scr[tile_rows(t), :] = v
            return carry

        lax.fori_loop(0, blk // SUBLANES, scatter, 0)

    @pl.when(b >= nblk)
    def _():
        rows = ya_ref.shape[0]
        t0 = pl.multiple_of((b - nblk) * rows, rows)
        routed = jnp.concatenate(
            [acc_scr[pl.ds(t0 * nchunk + c, rows, stride=nchunk), :] for c in range(nchunk)], axis=1)
        hs = h2_ref[...].astype(BF16)
        hid = (_silu(_dot(hs, wsg_ref[...])) * _dot(hs, wsu_ref[...])).astype(BF16)
        f = routed + _dot(hid, wsd_ref[...])
        g2 = g2_ref[0]
        g2 = jnp.broadcast_to(g2[:, None, :], (rows // ts, ts, d)).reshape(rows, d)
        y = x1_ref[...] + g2 * _rms(f, npost_ref[...])
        tile = p * n_epi + (b - nblk)

        @pl.when(tile < tiles_a)
        def _():
            ya_ref[...] = y

        @pl.when(tile >= tiles_a)
        def _():
            yb_ref[...] = y


def _moe(h2, x1, g2_tiles, npost, plan, w_gate, w_up, w_down, wsg, wsu, wsd, *, parts, blk, ts, n_a):
    tok, wts, blk_e, blk_valid, blk_new, nblk = plan
    n, d = h2.shape
    npart = n // parts
    ne, _, ff = w_gate.shape
    epi = LANES
    n_epi = npart // epi
    tiles_a = n_a // epi
    tiles_b = (n - n_a) // epi
    h3 = h2.reshape(n, d // LANES, LANES)
    kern = functools.partial(_moe_kernel, nblk=nblk, blk=blk, ts=ts, n_epi=n_epi, tiles_a=tiles_a)
    blk_of = lambda p, b: p * nblk + jnp.minimum(b, nblk - 1)
    epi_of = lambda p, b: p * n_epi + jnp.clip(b - nblk, 0, n_epi - 1)
    exp_of = lambda p, b, be: be[p, jnp.minimum(b, nblk - 1)]
    full = lambda a: pl.BlockSpec(a.shape, lambda p, b, be, bv, bn: (0,) * a.ndim)
    grid_spec = pltpu.PrefetchScalarGridSpec(
        num_scalar_prefetch=3,
        grid=(parts, nblk + n_epi),
        in_specs=[
            pl.BlockSpec((1, 1, blk), lambda p, b, be, bv, bn: (blk_of(p, b), 0, 0), memory_space=pltpu.SMEM),
            pl.BlockSpec((1, 1, blk), lambda p, b, be, bv, bn: (blk_of(p, b), 0, 0), memory_space=pltpu.SMEM),
            pl.BlockSpec((npart, d // LANES, LANES), lambda p, b, be, bv, bn: (p, 0, 0),
                         pipeline_mode=pl.Buffered(1)),
            pl.BlockSpec((epi, d), lambda p, b, be, bv, bn: (epi_of(p, b), 0)),
            pl.BlockSpec((epi, d), lambda p, b, be, bv, bn: (epi_of(p, b), 0)),
            pl.BlockSpec((1, epi // ts, d), lambda p, b, be, bv, bn: (epi_of(p, b), 0, 0)),
            full(npost),
            pl.BlockSpec((1, d, ff), lambda p, b, be, bv, bn: (exp_of(p, b, be), 0, 0)),
            pl.BlockSpec((1, d, ff), lambda p, b, be, bv, bn: (exp_of(p, b, be), 0, 0)),
            pl.BlockSpec((1, ff, d), lambda p, b, be, bv, bn: (exp_of(p, b, be), 0, 0)),
            full(wsg), full(wsu), full(wsd),
        ],
        out_specs=[
            pl.BlockSpec((epi, d), lambda p, b, be, bv, bn: (jnp.minimum(epi_of(p, b), tiles_a - 1), 0)),
            pl.BlockSpec((epi, d), lambda p, b, be, bv, bn: (jnp.clip(epi_of(p, b) - tiles_a, 0, tiles_b - 1), 0))],
        scratch_shapes=[pltpu.VMEM(((npart + 1) * (d // LANES), LANES), F32),
                        pltpu.VMEM((blk * (d // LANES), LANES), F32),
                        pltpu.VMEM((blk * (d // LANES), LANES), F32),
                        pltpu.VMEM((blk, d), BF16),
                        pltpu.VMEM((d, 2 * ff), BF16),
                        pltpu.VMEM((ff, d), BF16)],
    )
    return pl.pallas_call(
        kern,
        grid_spec=grid_spec,
        out_shape=[jax.ShapeDtypeStruct((n_a, d), F32), jax.ShapeDtypeStruct((n - n_a, d), F32)],
        compiler_params=pltpu.CompilerParams(
            dimension_semantics=("arbitrary", "arbitrary"), vmem_limit_bytes=VMEM_LIMIT_BYTES),
        name="moe_experts",
    )(blk_e, blk_valid, blk_new, tok, wts, h3, h2, x1, g2_tiles, npost, w_gate, w_up, w_down, wsg, wsu, wsd)


def _rope_tables(pos):
    half = 64
    inv_freq = ROPE_BASE ** (-jnp.arange(half, dtype=F32) / half)
    ang = pos[:, None] * inv_freq[None, :]
    cos, sin = jnp.cos(ang), jnp.sin(ang)
    return jnp.concatenate([cos, cos], axis=1), jnp.concatenate([-sin, sin], axis=1)


def _mixer_params(norm_pre_mix, norm_post_mix, norm_pre_ffn, w_in, gla_gate_up, gla_gate_bias,
                  ret_norm_w, ret_norm_b, gla_norm_w, w_out, w_router):
    d, in_w = w_in.shape
    pad = (-in_w) % LANES
    rank = gla_gate_up.shape[0]
    wr_hi, wr_lo = _split2(w_router.T)
    return dict(
        ret_heads=ret_norm_w.shape[0], gla_heads=gla_norm_w.shape[0],
        npre=norm_pre_mix.reshape(1, d), npost=norm_post_mix.reshape(1, d), npre2=norm_pre_ffn.reshape(1, d),
        gbias=gla_gate_bias.reshape(1, -1), rnw=ret_norm_w.reshape(1, -1), rnb=ret_norm_b.reshape(1, -1),
        gnw=gla_norm_w.reshape(1, -1),
        w_in=jnp.pad(w_in, ((0, 0), (0, pad))).astype(BF16),
        gup=jnp.pad(gla_gate_up, ((0, LANES - rank), (0, 0))).astype(BF16),
        w_out=w_out.astype(BF16), wr_hi=wr_hi, wr_lo=wr_lo)


def kernel(x_prompt, x_sample, state_ret, state_gla, c_prompt, c_sample, w_ada, b_ada, norm_pre_mix, norm_post_mix, norm_pre_ffn, norm_post_ffn, w_in, gla_gate_up, gla_gate_bias, ret_norm_w, ret_norm_b, gla_norm_w, w_out, w_router, router_bias, w_exp_gate, w_exp_up, w_exp_down, w_sh_gate, w_sh_up, w_sh_down):
    bp, tp, d = x_prompt.shape
    bs, ts, _ = x_sample.shape
    l = 0
    c_all = jnp.concatenate([c_prompt, c_sample], axis=0)
    mod = _ada(c_all, w_ada[l], b_ada[l]).reshape(bp + bs, 6, d)
    p = _mixer_params(norm_pre_mix[l], norm_post_mix[l], norm_pre_ffn[l], w_in[l], gla_gate_up[l],
                      gla_gate_bias[l], ret_norm_w[l], ret_norm_b[l], gla_norm_w[l], w_out[l], w_router[l])
    n_p, n_s = bp * tp, bs * ts
    n = n_p + n_s
    cos_p, sin_p = _rope_tables(jnp.arange(tp, dtype=F32))
    x1, h2, logits_t, sret_p, sgla_p = _prompt_mixer(x_prompt, mod[:bp], p, cos_p, sin_p,
                                                     tile=PROMPT_TILE, n_total=n)
    cos_s, sin_s = _rope_tables(PAST_LEN + jnp.arange(ts, dtype=F32))
    cos_s, sin_s = jnp.tile(cos_s, (SAMPLE_TILE // ts, 1)), jnp.tile(sin_s, (SAMPLE_TILE // ts, 1))
    x1, h2, logits_t, sret_s, sgla_s = _sample_mixer(
        x_sample.reshape(n_s, d), mod[bp:].transpose(1, 0, 2), p, cos_s, sin_s, state_ret[l], state_gla[l],
        x1, h2, logits_t, ts=ts, tile=SAMPLE_TILE)
    gh, gdk = p["gla_heads"], p["gup"].shape[1] // p["gla_heads"]
    sgla_blocks = sgla_p.reshape(bp, gh, 128, gh, gdk)
    new_gla_p = jnp.stack([sgla_blocks[:, h, :, h, :] for h in range(gh)], axis=1).transpose(0, 1, 3, 2)

    idx_t, w_t, cnt = _router(logits_t, router_bias[l], parts=MOE_PARTS, tile=ROUTER_TILE)
    plan = _dispatch_plan(idx_t, w_t, cnt[:, :, 0].astype(jnp.int32), parts=MOE_PARTS, blk=MOE_BLOCK)
    per_tile = LANES // ts
    g2 = mod[:, 5, :]
    g2_tiles = jnp.concatenate(
        [jnp.broadcast_to(jnp.repeat(g2[:bp], tp // LANES, axis=0)[:, None, :], (n_p // LANES, per_tile, d)),
         g2[bp:].reshape(n_s // LANES, per_tile, d)], axis=0)
    y_p, y_s = _moe(h2, x1, g2_tiles, norm_post_ffn[l].reshape(1, d), plan,
                    w_exp_gate[l], w_exp_up[l], w_exp_down[l],
                    w_sh_gate[l].astype(BF16), w_sh_up[l].astype(BF16), w_sh_down[l].astype(BF16),
                    parts=MOE_PARTS, blk=MOE_BLOCK, ts=ts, n_a=n_p)
    return (y_p.reshape(bp, tp, d), y_s.reshape(bs, ts, d), sret_p[None], new_gla_p[None],
            sret_s[None], sgla_s[None])
```

```python
import functools
import math

import jax
import jax.numpy as jnp
from jax import lax
from jax.experimental import pallas as pl
from jax.experimental.pallas import tpu as pltpu

F32 = jnp.float32
BF16 = jnp.bfloat16

PAST_LEN = 16384
ROPE_BASE = 10000.0
GLA_GATE_NORM = 16.0
TOP_K = 8
N_GROUPS = 8
TOPK_GROUPS = 4
ROUTED_SCALE = 2.5
EPS = 1e-6

LANES = 128
SUBLANES = 8
VMEM_LIMIT_BYTES = 56 * 1024 * 1024

GLA_CHUNK = 64
PROMPT_TILE = 256
SAMPLE_TILE = 128
ROUTER_TILE = 256
MOE_PARTS = 4
MOE_BLOCK = 128


def _dot(a, b):
    return jnp.dot(a, b, preferred_element_type=F32)


def _dot_nt(a, b):
    return lax.dot_general(a, b, (((1,), (1,)), ((), ())), preferred_element_type=F32)


def _dot_tn(a, b):
    return lax.dot_general(a, b, (((0,), (0,)), ((), ())), preferred_element_type=F32)


def _split2(x):
    hi = x.astype(BF16)
    lo = (x - hi.astype(F32)).astype(BF16)
    return hi, lo


def _split3(x):
    hi = x.astype(BF16)
    r = x - hi.astype(F32)
    mid = r.astype(BF16)
    lo = (r - mid.astype(F32)).astype(BF16)
    return hi, mid, lo


def _rms(x, g):
    return x * lax.rsqrt(jnp.mean(x * x, axis=-1, keepdims=True) + EPS) * g


def _silu(x):
    return x * jax.nn.sigmoid(x)


def _log_sigmoid(x):
    return jnp.minimum(x, 0.0) - jnp.log(1.0 + jnp.exp(-jnp.abs(x)))


def _ada_kernel(c_ref, w_ref, b_ref, o_ref):
    a_hi, a_lo = _split2(_silu(c_ref[...]))
    w_hi, w_lo = _split2(w_ref[...])
    o_ref[...] = _dot(a_hi, w_hi) + _dot(a_hi, w_lo) + _dot(a_lo, w_hi) + b_ref[...]


def _ada(c, w, b):
    rows, d = c.shape
    n = w.shape[1]
    tn = 1024
    return pl.pallas_call(
        _ada_kernel,
        grid=(n // tn,),
        in_specs=[
            pl.BlockSpec((rows, d), lambda j: (0, 0)),
            pl.BlockSpec((d, tn), lambda j: (0, j)),
            pl.BlockSpec((1, tn), lambda j: (0, j)),
        ],
        out_specs=pl.BlockSpec((rows, tn), lambda j: (0, j)),
        out_shape=jax.ShapeDtypeStruct((rows, n), F32),
        compiler_params=pltpu.CompilerParams(
            dimension_semantics=("arbitrary",), vmem_limit_bytes=VMEM_LIMIT_BYTES),
        name="ada_mod",
    )(c, w, b.reshape(1, n))


def _rotary(x, cos, sin_signed):
    return x * cos + pltpu.roll(x, x.shape[-1] // 2, axis=1) * sin_signed


def _level_reference(bc, s):
    rows, width = bc.shape
    pieces = []
    sub = lax.broadcasted_iota(jnp.int32, (SUBLANES, width), 0)
    for g in range(rows // SUBLANES):
        base = g * SUBLANES
        bounds = sorted({((base + r) // (2 * s)) * (2 * s) + s - 1 for r in range(SUBLANES)})
        piece = jnp.broadcast_to(bc[bounds[-1]:bounds[-1] + 1, :], (SUBLANES, width))
        for bm in reversed(bounds[:-1]):
            last_row_of_pair = bm + s - base
            piece = jnp.where(sub <= last_row_of_pair,
                              jnp.broadcast_to(bc[bm:bm + 1, :], (SUBLANES, width)), piece)
        pieces.append(piece)
    return jnp.concatenate(pieces, axis=0)


def _gla_level_masks(rows, heads, levels):
    i = lax.broadcasted_iota(jnp.int32, (rows, heads * rows), 0)
    j = lax.broadcasted_iota(jnp.int32, (rows, heads * rows), 1) & (rows - 1)
    masks = {}
    for s in levels:
        if s == 0:
            masks[s] = i == j
        else:
            sh = int(math.log2(s))
            masks[s] = ((i >> sh) == (j >> sh) + 1) & ((i >> (sh + 1)) == (j >> (sh + 1)))
    return masks


def _gla_intra_scores(q, k, bc, levels, masks, head_masks):
    scores = None
    for s in levels:
        if s == 0:
            qt, kt = q, k
        else:
            ref = _level_reference(bc, s)
            qt = q * jnp.exp(jnp.minimum(bc - ref, 0.0))
            kt = k * jnp.exp(jnp.minimum(ref - bc, 0.0))
        rhs_t = _block_diag_rows(kt.astype(BF16), head_masks)
        sc = jnp.where(masks[s], _dot_nt(qt.astype(BF16), rhs_t), 0.0)
        scores = sc if scores is None else scores + sc
    return scores


def _block_diag_rows(x, col_masks):
    return jnp.concatenate([x * m for m in col_masks], axis=0)


def _head_masks(rows, heads, width):
    lane = lax.broadcasted_iota(jnp.int32, (rows, heads * width), 1)
    sh = int(math.log2(width))
    return [jnp.where((lane >> sh) == h, 1.0, 0.0).astype(BF16) for h in range(heads)]


def _mixer_tail(x, mix_bf16, w_out, g1, npost, npre2, sc2, sh2, wr_hi, wr_lo):
    mix = _dot(mix_bf16, w_out)
    x1 = x + g1 * _rms(mix, npost)
    h2 = _rms(x1, npre2) * (1.0 + sc2) + sh2
    h_hi, h_lo = _split2(h2)
    logits_t = _dot_nt(wr_hi, h_hi) + _dot_nt(wr_hi, h_lo) + _dot_nt(wr_lo, h_hi)
    return x1, h2, logits_t


def _prompt_mixer_kernel(*refs, n_seq, ret_heads, gla_heads):
    x1_ref, h2_ref, lg_ref = refs[16:19]

    @pl.when(pl.program_id(0) < n_seq)
    def _():
        _prompt_mixer_body(*refs, ret_heads=ret_heads, gla_heads=gla_heads)

    @pl.when(pl.program_id(0) >= n_seq)
    def _():
        x1_ref[...] = jnp.zeros_like(x1_ref)
        h2_ref[...] = jnp.zeros_like(h2_ref)
        lg_ref[...] = jnp.zeros_like(lg_ref)


def _prompt_mixer_body(x_ref, mod_ref, npre_ref, npost_ref, npre2_ref, gbias_ref, rnw_ref, rnb_ref,
                       gnw_ref, win_ref, gup_ref, wout_ref, wrhi_ref, wrlo_ref, cos_ref, sin_ref,
                       x1_ref, h2_ref, lg_ref, sret_ref, sgla_ref,
                       proj_scr, mix_scr, dmat_scr, sret_scr, sgla_scr, *, ret_heads, gla_heads):
    t = pl.program_id(1)
    tt = x_ref.shape[1]
    rdk = 128
    rdv = 128
    gdk = gup_ref.shape[1] // gla_heads
    gdv = 128
    rq0, rk0, rv0, rg0 = 0, ret_heads * rdk, 2 * ret_heads * rdk, 2 * ret_heads * rdk + ret_heads * rdv
    gq0 = rg0 + ret_heads * rdv
    gk0 = gq0 + gla_heads * gdk
    gv0 = gk0 + gla_heads * gdk
    gg0 = gv0 + gla_heads * gdv
    ga0 = gg0 + gla_heads * gdv
    log_gamma = [math.log(1.0 - 2.0 ** (-5.0 - h)) for h in range(ret_heads)]

    @pl.when(jnp.logical_and(pl.program_id(0) == 0, t == 0))
    def _():
        i = lax.broadcasted_iota(jnp.int32, (tt, tt), 0)
        j = lax.broadcasted_iota(jnp.int32, (tt, tt), 1)
        for h in range(ret_heads):
            dmat_scr[h] = jnp.where(i >= j, jnp.exp(jnp.where(i >= j, (i - j).astype(F32) * log_gamma[h], 0.0)), 0.0)

    @pl.when(t == 0)
    def _():
        sret_scr[...] = jnp.zeros_like(sret_scr)
        sgla_scr[...] = jnp.zeros_like(sgla_scr)

    x = x_ref[0]
    mod = mod_ref[0]
    sh1, sc1, g1, sh2, sc2 = mod[0:1], mod[1:2], mod[2:3], mod[3:4], mod[4:5]
    h = _rms(x, npre_ref[...]) * (1.0 + sc1) + sh1
    proj_scr[...] = _dot(h.astype(BF16), win_ref[...])

    cos = cos_ref[...]
    sin = sin_ref[...]
    row = lax.broadcasted_iota(jnp.int32, (tt, rdk), 0).astype(F32)
    for hd in range(ret_heads):
        lg = log_gamma[hd]
        q = _rotary(proj_scr[:, rq0 + hd * rdk: rq0 + (hd + 1) * rdk], cos, sin)
        k = _rotary(proj_scr[:, rk0 + hd * rdk: rk0 + (hd + 1) * rdk], cos, sin) * (rdk ** -0.5)
        v = proj_scr[:, rv0 + hd * rdv: rv0 + (hd + 1) * rdv].astype(BF16)
        g = proj_scr[:, rg0 + hd * rdv: rg0 + (hd + 1) * rdv]
        qb = q.astype(BF16)
        s_old = sret_scr[hd]
        scores = (_dot_nt(qb, k.astype(BF16)) * dmat_scr[hd]).astype(BF16)
        o = _dot(scores, v) + jnp.exp((row + 1.0) * lg) * _dot(qb, s_old.astype(BF16))
        k_dec = (k * jnp.exp((tt - 1.0 - row) * lg)).astype(BF16)
        sret_scr[hd] = s_old * math.exp(tt * lg) + _dot_tn(k_dec, v)
        mu = jnp.mean(o, axis=-1, keepdims=True)
        oc = o - mu
        var = jnp.mean(oc * oc, axis=-1, keepdims=True)
        y = oc * lax.rsqrt(var + EPS) * rnw_ref[:, hd * rdv:(hd + 1) * rdv] + rnb_ref[:, hd * rdv:(hd + 1) * rdv]
        mix_scr[:, hd * rdv:(hd + 1) * rdv] = (_silu(g) * y).astype(BF16)

    c = GLA_CHUNK
    n_chunks = tt // c
    gw = gla_heads * gdk
    ga = proj_scr[:, ga0:ga0 + LANES].astype(BF16)
    logit = _dot(ga, gup_ref[...]) + gbias_ref[...]
    la = _log_sigmoid(logit) * (1.0 / GLA_GATE_NORM)
    ri = lax.broadcasted_iota(jnp.int32, (tt, tt), 0)
    ci = lax.broadcasted_iota(jnp.int32, (tt, tt), 1)
    csh = int(math.log2(c))
    tril = jnp.where((ri >= ci) & ((ri >> csh) == (ci >> csh)), 1.0, 0.0).astype(BF16)
    la_hi, la_mid, la_lo = _split3(la)
    bcum = _dot(tril, la_hi) + _dot(tril, la_mid) + _dot(tril, la_lo)
    levels = [s for s in (32, 16, 8, 4, 2, 1, 0) if s < c]
    masks = _gla_level_masks(c, gla_heads, levels)
    hm_k = _head_masks(c, gla_heads, gdk)
    hm_v = _head_masks(c, gla_heads, gdv)
    bd_mask = (lax.broadcasted_iota(jnp.int32, (gla_heads * gdv, gw), 0) >> int(math.log2(gdv))) == (
        lax.broadcasted_iota(jnp.int32, (gla_heads * gdv, gw), 1) >> int(math.log2(gdk)))
    for ch in range(n_chunks):
        r0 = ch * c
        q = proj_scr[r0:r0 + c, gq0:gq0 + gw] * (gdk ** -0.5)
        k = proj_scr[r0:r0 + c, gk0:gk0 + gw]
        v = proj_scr[r0:r0 + c, gv0:gv0 + gla_heads * gdv].astype(BF16)
        bc = bcum[r0:r0 + c]
        scores = _gla_intra_scores(q, k, bc, levels, masks, hm_k)
        o = _dot(scores.astype(BF16), _block_diag_rows(v, hm_v))
        st = sgla_scr[...]
        o = o + _dot_nt((q * jnp.exp(bc)).astype(BF16), st.astype(BF16))
        b_last = bc[c - 1:c, :]
        k_dec = (k * jnp.exp(b_last - bc)).astype(BF16)
        sgla_scr[...] = st * jnp.exp(b_last) + jnp.where(bd_mask, _dot_tn(v, k_dec), 0.0)
        gg = proj_scr[r0:r0 + c, gg0:gg0 + gla_heads * gdv]
        for hd in range(gla_heads):
            oh = o[:, hd * gdv:(hd + 1) * gdv]
            y = oh * lax.rsqrt(jnp.mean(oh * oh, axis=-1, keepdims=True) + EPS) * gnw_ref[:, hd * gdv:(hd + 1) * gdv]
            mix_scr[r0:r0 + c, ret_heads * rdv + hd * gdv: ret_heads * rdv + (hd + 1) * gdv] = (
                _silu(gg[:, hd * gdv:(hd + 1) * gdv]) * y).astype(BF16)

    x1, h2, logits_t = _mixer_tail(x, mix_scr[...], wout_ref[...], g1, npost_ref[...], npre2_ref[...],
                                   sc2, sh2, wrhi_ref[...], wrlo_ref[...])
    x1_ref[...] = x1
    h2_ref[...] = h2
    lg_ref[...] = logits_t

    @pl.when(t == pl.num_programs(1) - 1)
    def _():
        sret_ref[0] = sret_scr[...]
        sgla_ref[0] = sgla_scr[...]


def _prompt_mixer(x, mod, p, cos, sin, *, tile, n_total):
    b, t, d = x.shape
    nt = t // tile
    tiles = n_total // tile
    extra = -(-(tiles - b * nt) // nt)
    rh, gh = p["ret_heads"], p["gla_heads"]
    in_w = p["w_in"].shape[1]
    ne = p["wr_hi"].shape[0]
    gw = p["gup"].shape[1]
    full = lambda a: pl.BlockSpec(a.shape, lambda i, j: (0,) * a.ndim)
    vecs = [p["npre"], p["npost"], p["npre2"], p["gbias"], p["rnw"], p["rnb"], p["gnw"]]
    mats = [p["w_in"], p["gup"], p["w_out"], p["wr_hi"], p["wr_lo"]]
    kern = functools.partial(_prompt_mixer_kernel, n_seq=b, ret_heads=rh, gla_heads=gh)
    seq = lambda i: jnp.minimum(i, b - 1)
    out_tile = lambda i, j: jnp.minimum(i * nt + j, tiles - 1)
    return pl.pallas_call(
        kern,
        grid=(b + extra, nt),
        in_specs=[pl.BlockSpec((1, tile, d), lambda i, j: (seq(i), j, 0)),
                  pl.BlockSpec((1, 6, d), lambda i, j: (seq(i), 0, 0))]
                 + [full(a) for a in vecs] + [full(a) for a in mats]
                 + [pl.BlockSpec((tile, LANES), lambda i, j: (j, 0)),
                    pl.BlockSpec((tile, LANES), lambda i, j: (j, 0))],
        out_specs=[pl.BlockSpec((tile, d), lambda i, j: (out_tile(i, j), 0)),
                   pl.BlockSpec((tile, d), lambda i, j: (out_tile(i, j), 0)),
                   pl.BlockSpec((ne, tile), lambda i, j: (0, out_tile(i, j))),
                   pl.BlockSpec((1, rh, 128, 128), lambda i, j: (seq(i), 0, 0, 0)),
                   pl.BlockSpec((1, gh * 128, gw), lambda i, j: (seq(i), 0, 0))],
        out_shape=[jax.ShapeDtypeStruct((n_total, d), F32),
                   jax.ShapeDtypeStruct((n_total, d), F32),
                   jax.ShapeDtypeStruct((ne, n_total), F32),
                   jax.ShapeDtypeStruct((b, rh, 128, 128), F32),
                   jax.ShapeDtypeStruct((b, gh * 128, gw), F32)],
        scratch_shapes=[pltpu.VMEM((tile, in_w), F32),
                        pltpu.VMEM((tile, p["w_out"].shape[0]), BF16),
                        pltpu.VMEM((rh, tile, tile), F32),
                        pltpu.VMEM((rh, 128, 128), F32),
                        pltpu.VMEM((gh * 128, gw), F32)],
        compiler_params=pltpu.CompilerParams(
            dimension_semantics=("arbitrary", "arbitrary"), vmem_limit_bytes=VMEM_LIMIT_BYTES),
        name="prompt_mixer",
    )(x, mod, *vecs, *mats, cos, sin)


def _sample_mixer_kernel(x_ref, mod_ref, npre_ref, npost_ref, npre2_ref, gbias_ref, rnw_ref, rnb_ref,
                         gnw_ref, win_ref, gup_ref, wout_ref, wrhi_ref, wrlo_ref, cos_ref, sin_ref,
                         sret_in_ref, sgla_in_ref, x1_any, h2_any, lg_any,
                         x1_ref, h2_ref, lg_ref, sret_ref, sgla_ref,
                         proj_scr, mix_scr, qrot_scr, krot_scr, oret_scr, bc_scr, gqe_scr, ogla_scr,
                         *, ret_heads, gla_heads, ts):
    del x1_any, h2_any, lg_any
    rows = x_ref.shape[0]
    nb = rows // ts
    rdk = 128
    rdv = 128
    gdk = gup_ref.shape[1] // gla_heads
    gdv = 128
    gw = gla_heads * gdk
    gvw = gla_heads * gdv
    rq0, rk0, rv0, rg0 = 0, ret_heads * rdk, 2 * ret_heads * rdk, 2 * ret_heads * rdk + ret_heads * rdv
    gq0 = rg0 + ret_heads * rdv
    gk0 = gq0 + gw
    gv0 = gk0 + gw
    gg0 = gv0 + gvw
    ga0 = gg0 + gvw
    log_gamma = [math.log(1.0 - 2.0 ** (-5.0 - h)) for h in range(ret_heads)]
    tsh = int(math.log2(ts))

    def per_row(m):
        return jnp.broadcast_to(m[:, None, :], (nb, ts, m.shape[-1])).reshape(rows, m.shape[-1])

    x = x_ref[...]
    sh1, sc1, g1, sh2, sc2 = (per_row(mod_ref[i]) for i in range(5))
    h = _rms(x, npre_ref[...]) * (1.0 + sc1) + sh1
    proj_scr[...] = _dot(h.astype(BF16), win_ref[...])

    cos = cos_ref[...]
    sin = sin_ref[...]
    ri = lax.broadcasted_iota(jnp.int32, (rows, rows), 0)
    ci = lax.broadcasted_iota(jnp.int32, (rows, rows), 1)
    same = (ri >= ci) & ((ri >> tsh) == (ci >> tsh))
    for hd in range(ret_heads):
        lg = log_gamma[hd]
        q = _rotary(proj_scr[:, rq0 + hd * rdk: rq0 + (hd + 1) * rdk], cos, sin)
        k = _rotary(proj_scr[:, rk0 + hd * rdk: rk0 + (hd + 1) * rdk], cos, sin) * (rdk ** -0.5)
        v = proj_scr[:, rv0 + hd * rdv: rv0 + (hd + 1) * rdv].astype(BF16)
        qrot_scr[:, hd * rdk:(hd + 1) * rdk] = q
        krot_scr[:, hd * rdk:(hd + 1) * rdk] = k
        dmat = jnp.where(same, jnp.exp(jnp.where(same, (ri - ci).astype(F32) * lg, 0.0)), 0.0)
        scores = (_dot_nt(q.astype(BF16), k.astype(BF16)) * dmat).astype(BF16)
        oret_scr[:, hd * rdv:(hd + 1) * rdv] = _dot(scores, v)

    ga = proj_scr[:, ga0:ga0 + LANES].astype(BF16)
    logit = _dot(ga, gup_ref[...]) + gbias_ref[...]
    la = _log_sigmoid(logit) * (1.0 / GLA_GATE_NORM)
    tril = jnp.where(same, 1.0, 0.0).astype(BF16)
    la_hi, la_mid, la_lo = _split3(la)
    bcum = _dot(tril, la_hi) + _dot(tril, la_mid) + _dot(tril, la_lo)
    bc_scr[...] = bcum
    c = min(GLA_CHUNK, rows)
    levels = [s for s in (32, 16, 8, 4, 2, 1, 0) if s < ts]
    masks = _gla_level_masks(c, gla_heads, levels)
    hm_k = _head_masks(c, gla_heads, gdk)
    hm_v = _head_masks(c, gla_heads, gdv)
    for ch in range(rows // c):
        r0 = ch * c
        q = proj_scr[r0:r0 + c, gq0:gq0 + gw] * (gdk ** -0.5)
        k = proj_scr[r0:r0 + c, gk0:gk0 + gw]
        v = proj_scr[r0:r0 + c, gv0:gv0 + gvw].astype(BF16)
        bc = bcum[r0:r0 + c]
        scores = _gla_intra_scores(q, k, bc, levels, masks, hm_k)
        ogla_scr[r0:r0 + c, :] = _dot(scores.astype(BF16), _block_diag_rows(v, hm_v))
        gqe_scr[r0:r0 + c, :] = q * jnp.exp(bc)

    trow = lax.broadcasted_iota(jnp.int32, (ts, rdk), 0).astype(F32)
    eye = lax.broadcasted_iota(jnp.int32, (gw, gw), 0) == lax.broadcasted_iota(jnp.int32, (gw, gw), 1)
    zero_blk = jnp.zeros((gdk, gdv), BF16)

    def element(b, carry):
        rs = pl.ds(pl.multiple_of(b * ts, ts), ts)
        for hd in range(ret_heads):
            lg = log_gamma[hd]
            q = qrot_scr[rs, hd * rdk:(hd + 1) * rdk]
            k = krot_scr[rs, hd * rdk:(hd + 1) * rdk]
            v = proj_scr[rs, rv0 + hd * rdv: rv0 + (hd + 1) * rdv]
            s_old = sret_in_ref[b, hd]
            oret_scr[rs, hd * rdv:(hd + 1) * rdv] += jnp.exp((trow + 1.0) * lg) * _dot(
                q.astype(BF16), s_old.astype(BF16))
            k_dec = (k * jnp.exp((ts - 1.0 - trow) * lg)).astype(BF16)
            sret_ref[b, hd] = s_old * math.exp(ts * lg) + _dot_tn(k_dec, v.astype(BF16))
        s_b = sgla_in_ref[b]
        s_bd = jnp.concatenate(
            [jnp.concatenate([s_b[hd].astype(BF16) if h2 == hd else zero_blk for h2 in range(gla_heads)], axis=1)
             for hd in range(gla_heads)], axis=0)
        ogla_scr[rs, :] += _dot(gqe_scr[rs, :].astype(BF16), s_bd)
        bc = bc_scr[rs, :]
        b_last = bc[ts - 1:ts, :]
        k_dec = (proj_scr[rs, gk0:gk0 + gw] * jnp.exp(b_last - bc)).astype(BF16)
        upd = _dot_tn(k_dec, proj_scr[rs, gv0:gv0 + gvw].astype(BF16))
        decay_col = jnp.sum(jnp.where(eye, jnp.broadcast_to(jnp.exp(b_last), (gw, gw)), 0.0), axis=1, keepdims=True)
        for hd in range(gla_heads):
            sgla_ref[b, hd] = (s_b[hd] * decay_col[hd * gdk:(hd + 1) * gdk]
                               + upd[hd * gdk:(hd + 1) * gdk, hd * gdv:(hd + 1) * gdv])
        return carry

    lax.fori_loop(0, nb, element, 0)

    for hd in range(ret_heads):
        o = oret_scr[:, hd * rdv:(hd + 1) * rdv]
        g = proj_scr[:, rg0 + hd * rdv: rg0 + (hd + 1) * rdv]
        mu = jnp.mean(o, axis=-1, keepdims=True)
        oc = o - mu
        var = jnp.mean(oc * oc, axis=-1, keepdims=True)
        y = oc * lax.rsqrt(var + EPS) * rnw_ref[:, hd * rdv:(hd + 1) * rdv] + rnb_ref[:, hd * rdv:(hd + 1) * rdv]
        mix_scr[:, hd * rdv:(hd + 1) * rdv] = (_silu(g) * y).astype(BF16)
    for hd in range(gla_heads):
        oh = ogla_scr[:, hd * gdv:(hd + 1) * gdv]
        gg = proj_scr[:, gg0 + hd * gdv: gg0 + (hd + 1) * gdv]
        y = oh * lax.rsqrt(jnp.mean(oh * oh, axis=-1, keepdims=True) + EPS) * gnw_ref[:, hd * gdv:(hd + 1) * gdv]
        mix_scr[:, ret_heads * rdv + hd * gdv: ret_heads * rdv + (hd + 1) * gdv] = (_silu(gg) * y).astype(BF16)

    x1, h2, logits_t = _mixer_tail(x, mix_scr[...], wout_ref[...], g1, npost_ref[...], npre2_ref[...],
                                   sc2, sh2, wrhi_ref[...], wrlo_ref[...])
    x1_ref[...] = x1
    h2_ref[...] = h2
    lg_ref[...] = logits_t


def _sample_mixer(x, mod_t, p, cos, sin, state_ret, state_gla, x1_buf, h2_buf, lg_buf, *, ts, tile):
    n, d = x.shape
    off = (x1_buf.shape[0] - n) // tile
    nb = tile // ts
    rh, gh = p["ret_heads"], p["gla_heads"]
    in_w = p["w_in"].shape[1]
    ne = p["wr_hi"].shape[0]
    gw = p["gup"].shape[1]
    gdk = gw // gh
    full = lambda a: pl.BlockSpec(a.shape, lambda i: (0,) * a.ndim)
    vecs = [p["npre"], p["npost"], p["npre2"], p["gbias"], p["rnw"], p["rnb"], p["gnw"]]
    mats = [p["w_in"], p["gup"], p["w_out"], p["wr_hi"], p["wr_lo"]]
    kern = functools.partial(_sample_mixer_kernel, ret_heads=rh, gla_heads=gh, ts=ts)
    n_in = 2 + len(vecs) + len(mats) + 4 + 3
    return pl.pallas_call(
        kern,
        grid=(n // tile,),
        in_specs=[pl.BlockSpec((tile, d), lambda i: (i, 0)),
                  pl.BlockSpec((6, nb, d), lambda i: (0, i, 0))]
                 + [full(a) for a in vecs] + [full(a) for a in mats]
                 + [full(cos), full(sin),
                    pl.BlockSpec((nb, rh, 128, 128), lambda i: (i, 0, 0, 0)),
                    pl.BlockSpec((nb, gh, gdk, 128), lambda i: (i, 0, 0, 0)),
                    pl.BlockSpec(memory_space=pl.ANY), pl.BlockSpec(memory_space=pl.ANY),
                    pl.BlockSpec(memory_space=pl.ANY)],
        out_specs=[pl.BlockSpec((tile, d), lambda i: (off + i, 0)),
                   pl.BlockSpec((tile, d), lambda i: (off + i, 0)),
                   pl.BlockSpec((ne, tile), lambda i: (0, off + i)),
                   pl.BlockSpec((nb, rh, 128, 128), lambda i: (i, 0, 0, 0)),
                   pl.BlockSpec((nb, gh, gdk, 128), lambda i: (i, 0, 0, 0))],
        out_shape=[jax.ShapeDtypeStruct(x1_buf.shape, F32),
                   jax.ShapeDtypeStruct(h2_buf.shape, F32),
                   jax.ShapeDtypeStruct(lg_buf.shape, F32),
                   jax.ShapeDtypeStruct(state_ret.shape, F32),
                   jax.ShapeDtypeStruct(state_gla.shape, F32)],
        input_output_aliases={n_in - 3: 0, n_in - 2: 1, n_in - 1: 2},
        scratch_shapes=[pltpu.VMEM((tile, in_w), F32),
                        pltpu.VMEM((tile, p["w_out"].shape[0]), BF16),
                        pltpu.VMEM((tile, rh * 128), F32),
                        pltpu.VMEM((tile, rh * 128), F32),
                        pltpu.VMEM((tile, rh * 128), F32),
                        pltpu.VMEM((tile, gw), F32),
                        pltpu.VMEM((tile, gw), F32),
                        pltpu.VMEM((tile, gh * 128), F32)],
        compiler_params=pltpu.CompilerParams(
            dimension_semantics=("arbitrary",), vmem_limit_bytes=VMEM_LIMIT_BYTES),
        name="sample_mixer",
    )(x, mod_t, *vecs, *mats, cos, sin, state_ret, state_gla, x1_buf, h2_buf, lg_buf)


def _router_kernel(lg_ref, bias_ref, idx_ref, w_ref, cnt_ref, cnt_scr, *, n_groups, topk_groups, top_k):
    ne, tn = lg_ref.shape
    gsz = ne // n_groups
    neg = -jnp.inf

    @pl.when(pl.program_id(1) == 0)
    def _():
        cnt_scr[...] = jnp.zeros_like(cnt_scr)

    scores = jax.nn.sigmoid(lg_ref[...])
    sel = scores + bias_ref[...][:, 0:1]
    sel3 = sel.reshape(n_groups, gsz, tn)
    mem = lax.broadcasted_iota(jnp.int32, (n_groups, gsz, tn), 1)
    m1 = jnp.max(sel3, axis=1, keepdims=True)
    first = jnp.min(jnp.where(sel3 == m1, mem, gsz), axis=1, keepdims=True)
    m2 = jnp.max(jnp.where(mem == first, neg, sel3), axis=1, keepdims=True)
    gscore = (m1 + m2).reshape(n_groups, tn)
    gi = lax.broadcasted_iota(jnp.int32, (n_groups, tn), 0)
    gsel = jnp.zeros((n_groups, tn), jnp.bool_)
    work = gscore
    for _ in range(topk_groups):
        mx = jnp.max(work, axis=0, keepdims=True)
        pick = gi == jnp.min(jnp.where(work == mx, gi, n_groups), axis=0, keepdims=True)
        gsel = jnp.logical_or(gsel, pick)
        work = jnp.where(pick, neg, work)
    emask = jnp.broadcast_to(gsel[:, None, :], (n_groups, gsz, tn)).reshape(ne, tn)
    ei = lax.broadcasted_iota(jnp.int32, (ne, tn), 0)
    work = jnp.where(emask, sel, neg)
    chosen_any = jnp.zeros((ne, tn), jnp.bool_)
    idx_rows, w_rows = [], []
    for _ in range(top_k):
        mx = jnp.max(work, axis=0, keepdims=True)
        first_e = jnp.min(jnp.where(work == mx, ei, ne), axis=0, keepdims=True)
        pick = ei == first_e
        chosen_any = jnp.logical_or(chosen_any, pick)
        work = jnp.where(pick, neg, work)
        idx_rows.append(first_e)
        w_rows.append(jnp.sum(jnp.where(pick, scores, 0.0), axis=0, keepdims=True))
    wsum = w_rows[0]
    for r in w_rows[1:]:
        wsum = wsum + r
    idx_ref[...] = jnp.concatenate(idx_rows, axis=0)
    w_ref[...] = jnp.concatenate(w_rows, axis=0) / wsum * ROUTED_SCALE
    cnt_scr[...] += _dot(jnp.where(chosen_any, 1.0, 0.0).astype(BF16), jnp.ones((tn, LANES), BF16))
    cnt_ref[0] = cnt_scr[...]


def _router(logits_t, bias, *, parts, tile):
    ne, n = logits_t.shape
    tiles = n // parts // tile
    kern = functools.partial(_router_kernel, n_groups=N_GROUPS, topk_groups=TOPK_GROUPS, top_k=TOP_K)
    return pl.pallas_call(
        kern,
        grid=(parts, tiles),
        in_specs=[pl.BlockSpec((ne, tile), lambda p, j: (0, p * tiles + j)),
                  pl.BlockSpec((ne, LANES), lambda p, j: (0, 0))],
        out_specs=[pl.BlockSpec((TOP_K, tile), lambda p, j: (0, p * tiles + j)),
                   pl.BlockSpec((TOP_K, tile), lambda p, j: (0, p * tiles + j)),
                   pl.BlockSpec((1, ne, LANES), lambda p, j: (p, 0, 0))],
        out_shape=[jax.ShapeDtypeStruct((TOP_K, n), jnp.int32),
                   jax.ShapeDtypeStruct((TOP_K, n), F32),
                   jax.ShapeDtypeStruct((parts, ne, LANES), F32)],
        scratch_shapes=[pltpu.VMEM((ne, LANES), F32)],
        compiler_params=pltpu.CompilerParams(
            dimension_semantics=("arbitrary", "arbitrary"), vmem_limit_bytes=VMEM_LIMIT_BYTES),
        name="router_topk",
    )(logits_t, jnp.broadcast_to(bias.reshape(ne, 1), (ne, LANES)))


def _dispatch_plan(idx_t, w_t, counts, *, parts, blk, rows_per_token):
    k, n = idx_t.shape
    npart = n // parts
    ne = counts.shape[1]
    stride = npart + blk
    big = ne * stride
    local = jnp.arange(n, dtype=jnp.int32) % npart
    to_parts = lambda a: a.reshape(k, parts, npart).transpose(1, 0, 2).reshape(parts, k * npart)
    keys = to_parts(idx_t * stride + local[None, :])
    wts = to_parts(w_t)
    need = (-counts) % blk
    j = jnp.arange(blk - 1, dtype=jnp.int32)
    pad_keys = jnp.where(j[None, None, :] < need[:, :, None],
                         jnp.arange(ne, dtype=jnp.int32)[None, :, None] * stride + npart + j[None, None, :], big)
    total = k * npart + ne * (blk - 1)
    nblk = -(-total // blk)
    fill = nblk * blk - total
    all_keys = jnp.concatenate([keys, pad_keys.reshape(parts, -1), jnp.full((parts, fill), big, jnp.int32)], axis=1)
    all_w = jnp.concatenate([wts, jnp.zeros((parts, nblk * blk - k * npart), F32)], axis=1)
    sk, sw = lax.sort((all_keys, all_w), dimension=1, num_keys=1)
    valid = sk < big
    slot_tok = sk % stride
    real = jnp.logical_and(valid, slot_tok < npart)
    acc_row = jnp.where(real, slot_tok, npart) * rows_per_token
    src_row = jnp.where(real, slot_tok, 0) * rows_per_token
    w = jnp.where(real, sw, 0.0)
    blocks_per_expert = (counts + blk - 1) // blk
    first_blk = jnp.concatenate([jnp.zeros((parts, 1), jnp.int32),
                                 jnp.cumsum(blocks_per_expert, axis=1, dtype=jnp.int32)], axis=1)
    return src_row.reshape(-1), acc_row.reshape(-1), w.reshape(-1), first_blk, nblk


def _moe_kernel(first_ref, src_hbm, dst_hbm, wts_hbm, h3_ref, h2_ref, x1_ref, g2_ref, npost_ref,
                wg_ref, wu_ref, wd_ref, wsg_ref, wsu_ref, wsd_ref, ya_ref, yb_ref,
                acc_scr, g_scr, y2_scr, xs_scr, wgu_scr, wdn_scr, src_smem, dst_smem, wts_smem, sem,
                *, ne, nblk, blk, ts, n_epi, tiles_a):
    p = pl.program_id(0)
    b = pl.program_id(1)
    d = h2_ref.shape[1]
    ff = wg_ref.shape[2]
    nchunk = d // LANES
    tile_at = lambda row: pl.ds(pl.multiple_of(row, nchunk), nchunk)

    @pl.when(b == 0)
    def _():
        acc_scr[...] = jnp.zeros_like(acc_scr)

    def slot_copies(k, slot):
        e = jnp.minimum(b, ne - 1)
        row = pl.multiple_of((p * nblk + first_ref[p, e] + k) * blk, blk)
        return tuple(pltpu.make_async_copy(hbm.at[pl.ds(row, blk)], smem.at[slot], sem.at[i, slot])
                     for i, (hbm, smem) in enumerate(((src_hbm, src_smem), (dst_hbm, dst_smem), (wts_hbm, wts_smem))))

    def expert_block(k, n_blocks):
        slot = k & 1
        for cp in slot_copies(k, slot):
            cp.wait()

        @pl.when(k + 1 < n_blocks)
        def _():
            for cp in slot_copies(k + 1, 1 - slot):
                cp.start()

        for r in range(blk):
            g_scr[r * nchunk:(r + 1) * nchunk, :] = h3_ref[tile_at(src_smem[slot, r]), :]
        for c in range(nchunk):
            xs_scr[:, c * LANES:(c + 1) * LANES] = g_scr[pl.ds(c, blk, stride=nchunk), :].astype(BF16)
        gu = _dot(xs_scr[...], wgu_scr[...])
        hid = (_silu(gu[:, :ff]) * gu[:, ff:]).astype(BF16)
        y = _dot(hid, wdn_scr[...])
        for c in range(nchunk):
            y2_scr[pl.ds(c, blk, stride=nchunk), :] = y[:, c * LANES:(c + 1) * LANES]
        for g in range(blk // SUBLANES):
            rows = [g * SUBLANES + i for i in range(SUBLANES)]
            dsts = [dst_smem[slot, r] for r in rows]
            new = [acc_scr[tile_at(t), :] + wts_smem[slot, r] * y2_scr[r * nchunk:(r + 1) * nchunk, :]
                   for t, r in zip(dsts, rows)]
            for t, v in zip(dsts, new):
                acc_scr[tile_at(t), :] = v
        return n_blocks

    @pl.when(b < ne)
    def _():
        e = jnp.minimum(b, ne - 1)
        n_blocks = first_ref[p, e + 1] - first_ref[p, e]

        @pl.when(n_blocks > 0)
        def _():
            wgu_scr[:, :ff] = wg_ref[0].astype(BF16)
            wgu_scr[:, ff:] = wu_ref[0].astype(BF16)
            wdn_scr[...] = wd_ref[0].astype(BF16)
            for cp in slot_copies(0, 0):
                cp.start()
            lax.fori_loop(0, n_blocks, expert_block, n_blocks)

    @pl.when(b >= ne)
    def _():
        rows = ya_ref.shape[0]
        t0 = pl.multiple_of((b - ne) * rows, rows)
        routed = jnp.concatenate(
            [acc_scr[pl.ds(t0 * nchunk + c, rows, stride=nchunk), :] for c in range(nchunk)], axis=1)
        hs = h2_ref[...].astype(BF16)
        hid = (_silu(_dot(hs, wsg_ref[...])) * _dot(hs, wsu_ref[...])).astype(BF16)
        f = routed + _dot(hid, wsd_ref[...])
        g2 = g2_ref[0]
        g2 = jnp.broadcast_to(g2[:, None, :], (rows // ts, ts, d)).reshape(rows, d)
        y = x1_ref[...] + g2 * _rms(f, npost_ref[...])
        tile = p * n_epi + (b - ne)

        @pl.when(tile < tiles_a)
        def _():
            ya_ref[...] = y

        @pl.when(tile >= tiles_a)
        def _():
            yb_ref[...] = y


def _moe(h2, x1, g2_tiles, npost, plan, w_gate, w_up, w_down, wsg, wsu, wsd, *, parts, blk, ts, n_a):
    src_row, acc_row, wts, first_blk, nblk = plan
    n, d = h2.shape
    npart = n // parts
    ne, _, ff = w_gate.shape
    epi = LANES
    n_epi = npart // epi
    tiles_a = n_a // epi
    tiles_b = (n - n_a) // epi
    nchunk = d // LANES
    h3 = h2.reshape(n * nchunk, LANES)
    kern = functools.partial(_moe_kernel, ne=ne, nblk=nblk, blk=blk, ts=ts, n_epi=n_epi, tiles_a=tiles_a)
    epi_of = lambda p, b: p * n_epi + jnp.clip(b - ne, 0, n_epi - 1)
    exp_of = lambda p, b: jnp.minimum(b, ne - 1)
    full = lambda a: pl.BlockSpec(a.shape, lambda p, b, fb: (0,) * a.ndim)
    grid_spec = pltpu.PrefetchScalarGridSpec(
        num_scalar_prefetch=1,
        grid=(parts, ne + n_epi),
        in_specs=[
            pl.BlockSpec(memory_space=pl.ANY),
            pl.BlockSpec(memory_space=pl.ANY),
            pl.BlockSpec(memory_space=pl.ANY),
            pl.BlockSpec((npart * nchunk, LANES), lambda p, b, fb: (p, 0), pipeline_mode=pl.Buffered(1)),
            pl.BlockSpec((epi, d), lambda p, b, fb: (epi_of(p, b), 0)),
            pl.BlockSpec((epi, d), lambda p, b, fb: (epi_of(p, b), 0)),
            pl.BlockSpec((1, epi // ts, d), lambda p, b, fb: (epi_of(p, b), 0, 0)),
            full(npost),
            pl.BlockSpec((1, d, ff), lambda p, b, fb: (exp_of(p, b), 0, 0)),
            pl.BlockSpec((1, d, ff), lambda p, b, fb: (exp_of(p, b), 0, 0)),
            pl.BlockSpec((1, ff, d), lambda p, b, fb: (exp_of(p, b), 0, 0)),
            full(wsg), full(wsu), full(wsd),
        ],
        out_specs=[
            pl.BlockSpec((epi, d), lambda p, b, fb: (jnp.minimum(epi_of(p, b), tiles_a - 1), 0)),
            pl.BlockSpec((epi, d), lambda p, b, fb: (jnp.clip(epi_of(p, b) - tiles_a, 0, tiles_b - 1), 0))],
        scratch_shapes=[pltpu.VMEM(((npart + 1) * (d // LANES), LANES), F32),
                        pltpu.VMEM((blk * (d // LANES), LANES), F32),
                        pltpu.VMEM((blk * (d // LANES), LANES), F32),
                        pltpu.VMEM((blk, d), BF16),
                        pltpu.VMEM((d, 2 * ff), BF16),
                        pltpu.VMEM((ff, d), BF16),
                        pltpu.SMEM((2, blk), jnp.int32),
                        pltpu.SMEM((2, blk), jnp.int32),
                        pltpu.SMEM((2, blk), F32),
                        pltpu.SemaphoreType.DMA((3, 2))],
    )
    return pl.pallas_call(
        kern,
        grid_spec=grid_spec,
        out_shape=[jax.ShapeDtypeStruct((n_a, d), F32), jax.ShapeDtypeStruct((n - n_a, d), F32)],
        compiler_params=pltpu.CompilerParams(
            dimension_semantics=("arbitrary", "arbitrary"), vmem_limit_bytes=VMEM_LIMIT_BYTES),
        name="moe_experts",
    )(first_blk, src_row, acc_row, wts, h3, h2, x1, g2_tiles, npost, w_gate, w_up, w_down, wsg, wsu, wsd)


def _rope_tables(pos):
    half = 64
    inv_freq = ROPE_BASE ** (-jnp.arange(half, dtype=F32) / half)
    ang = pos[:, None] * inv_freq[None, :]
    cos, sin = jnp.cos(ang), jnp.sin(ang)
    return jnp.concatenate([cos, cos], axis=1), jnp.concatenate([-sin, sin], axis=1)


def _mixer_params(norm_pre_mix, norm_post_mix, norm_pre_ffn, w_in, gla_gate_up, gla_gate_bias,
                  ret_norm_w, ret_norm_b, gla_norm_w, w_out, w_router):
    d, in_w = w_in.shape
    pad = (-in_w) % LANES
    rank = gla_gate_up.shape[0]
    wr_hi, wr_lo = _split2(w_router.T)
    return dict(
        ret_heads=ret_norm_w.shape[0], gla_heads=gla_norm_w.shape[0],
        npre=norm_pre_mix.reshape(1, d), npost=norm_post_mix.reshape(1, d), npre2=norm_pre_ffn.reshape(1, d),
        gbias=gla_gate_bias.reshape(1, -1), rnw=ret_norm_w.reshape(1, -1), rnb=ret_norm_b.reshape(1, -1),
        gnw=gla_norm_w.reshape(1, -1),
        w_in=jnp.pad(w_in, ((0, 0), (0, pad))).astype(BF16),
        gup=jnp.pad(gla_gate_up, ((0, LANES - rank), (0, 0))).astype(BF16),
        w_out=w_out.astype(BF16), wr_hi=wr_hi, wr_lo=wr_lo)


def kernel(x_prompt, x_sample, state_ret, state_gla, c_prompt, c_sample, w_ada, b_ada, norm_pre_mix, norm_post_mix, norm_pre_ffn, norm_post_ffn, w_in, gla_gate_up, gla_gate_bias, ret_norm_w, ret_norm_b, gla_norm_w, w_out, w_router, router_bias, w_exp_gate, w_exp_up, w_exp_down, w_sh_gate, w_sh_up, w_sh_down):
    bp, tp, d = x_prompt.shape
    bs, ts, _ = x_sample.shape
    l = 0
    c_all = jnp.concatenate([c_prompt, c_sample], axis=0)
    mod = _ada(c_all, w_ada[l], b_ada[l]).reshape(bp + bs, 6, d)
    p = _mixer_params(norm_pre_mix[l], norm_post_mix[l], norm_pre_ffn[l], w_in[l], gla_gate_up[l],
                      gla_gate_bias[l], ret_norm_w[l], ret_norm_b[l], gla_norm_w[l], w_out[l], w_router[l])
    n_p, n_s = bp * tp, bs * ts
    n = n_p + n_s
    cos_p, sin_p = _rope_tables(jnp.arange(tp, dtype=F32))
    x1, h2, logits_t, sret_p, sgla_p = _prompt_mixer(x_prompt, mod[:bp], p, cos_p, sin_p,
                                                     tile=PROMPT_TILE, n_total=n)
    cos_s, sin_s = _rope_tables(PAST_LEN + jnp.arange(ts, dtype=F32))
    cos_s, sin_s = jnp.tile(cos_s, (SAMPLE_TILE // ts, 1)), jnp.tile(sin_s, (SAMPLE_TILE // ts, 1))
    x1, h2, logits_t, sret_s, sgla_s = _sample_mixer(
        x_sample.reshape(n_s, d), mod[bp:].transpose(1, 0, 2), p, cos_s, sin_s, state_ret[l], state_gla[l],
        x1, h2, logits_t, ts=ts, tile=SAMPLE_TILE)
    gh, gdk = p["gla_heads"], p["gup"].shape[1] // p["gla_heads"]
    sgla_blocks = sgla_p.reshape(bp, gh, 128, gh, gdk)
    new_gla_p = jnp.stack([sgla_blocks[:, h, :, h, :] for h in range(gh)], axis=1).transpose(0, 1, 3, 2)

    idx_t, w_t, cnt = _router(logits_t, router_bias[l], parts=MOE_PARTS, tile=ROUTER_TILE)
    plan = _dispatch_plan(idx_t, w_t, cnt[:, :, 0].astype(jnp.int32), parts=MOE_PARTS, blk=MOE_BLOCK,
                          rows_per_token=d // LANES)
    per_tile = LANES // ts
    g2 = mod[:, 5, :]
    g2_tiles = jnp.concatenate(
        [jnp.broadcast_to(jnp.repeat(g2[:bp], tp // LANES, axis=0)[:, None, :], (n_p // LANES, per_tile, d)),
         g2[bp:].reshape(n_s // LANES, per_tile, d)], axis=0)
    y_p, y_s = _moe(h2, x1, g2_tiles, norm_post_ffn[l].reshape(1, d), plan,
                    w_exp_gate[l], w_exp_up[l], w_exp_down[l],
                    w_sh_gate[l].astype(BF16), w_sh_up[l].astype(BF16), w_sh_down[l].astype(BF16),
                    parts=MOE_PARTS, blk=MOE_BLOCK, ts=ts, n_a=n_p)
    return (y_p.reshape(bp, tp, d), y_s.reshape(bs, ts, d), sret_p[None], new_gla_p[None],
            sret_s[None], sgla_s[None])
```

```python
import functools
import math

import jax
import jax.numpy as jnp
from jax import lax
from jax.experimental import pallas as pl
from jax.experimental.pallas import tpu as pltpu

F32 = jnp.float32
BF16 = jnp.bfloat16

PAST_LEN = 16384
ROPE_BASE = 10000.0
GLA_GATE_NORM = 16.0
TOP_K = 8
N_GROUPS = 8
TOPK_GROUPS = 4
ROUTED_SCALE = 2.5
EPS = 1e-6

LANES = 128
SUBLANES = 8
VMEM_LIMIT_BYTES = 56 * 1024 * 1024

GLA_CHUNK = 64
PROMPT_TILE = 256
SAMPLE_TILE = 128
ROUTER_TILE = 256
MOE_PARTS = 4
MOE_BLOCK = 128
MOE_TABLE_DEPTH = 4


def _dot(a, b):
    return jnp.dot(a, b, preferred_element_type=F32)


def _dot_nt(a, b):
    return lax.dot_general(a, b, (((1,), (1,)), ((), ())), preferred_element_type=F32)


def _dot_tn(a, b):
    return lax.dot_general(a, b, (((0,), (0,)), ((), ())), preferred_element_type=F32)


def _split2(x):
    hi = x.astype(BF16)
    lo = (x - hi.astype(F32)).astype(BF16)
    return hi, lo


def _split3(x):
    hi = x.astype(BF16)
    r = x - hi.astype(F32)
    mid = r.astype(BF16)
    lo = (r - mid.astype(F32)).astype(BF16)
    return hi, mid, lo


def _rms(x, g):
    return x * lax.rsqrt(jnp.mean(x * x, axis=-1, keepdims=True) + EPS) * g


def _silu(x):
    return x * jax.nn.sigmoid(x)


def _log_sigmoid(x):
    return jnp.minimum(x, 0.0) - jnp.log(1.0 + jnp.exp(-jnp.abs(x)))


def _ada_kernel(c_ref, w_ref, b_ref, o_ref):
    a_hi, a_lo = _split2(_silu(c_ref[...]))
    w_hi, w_lo = _split2(w_ref[...])
    o_ref[...] = _dot(a_hi, w_hi) + _dot(a_hi, w_lo) + _dot(a_lo, w_hi) + b_ref[...]


def _ada(c, w, b):
    rows, d = c.shape
    n = w.shape[1]
    tn = 1024
    return pl.pallas_call(
        _ada_kernel,
        grid=(n // tn,),
        in_specs=[
            pl.BlockSpec((rows, d), lambda j: (0, 0)),
            pl.BlockSpec((d, tn), lambda j: (0, j)),
            pl.BlockSpec((1, tn), lambda j: (0, j)),
        ],
        out_specs=pl.BlockSpec((rows, tn), lambda j: (0, j)),
        out_shape=jax.ShapeDtypeStruct((rows, n), F32),
        compiler_params=pltpu.CompilerParams(
            dimension_semantics=("arbitrary",), vmem_limit_bytes=VMEM_LIMIT_BYTES),
        name="ada_mod",
    )(c, w, b.reshape(1, n))


def _rotary(x, cos, sin_signed):
    return x * cos + pltpu.roll(x, x.shape[-1] // 2, axis=1) * sin_signed


def _level_reference(bc, s):
    rows, width = bc.shape
    pieces = []
    sub = lax.broadcasted_iota(jnp.int32, (SUBLANES, width), 0)
    for g in range(rows // SUBLANES):
        base = g * SUBLANES
        bounds = sorted({((base + r) // (2 * s)) * (2 * s) + s - 1 for r in range(SUBLANES)})
        piece = jnp.broadcast_to(bc[bounds[-1]:bounds[-1] + 1, :], (SUBLANES, width))
        for bm in reversed(bounds[:-1]):
            last_row_of_pair = bm + s - base
            piece = jnp.where(sub <= last_row_of_pair,
                              jnp.broadcast_to(bc[bm:bm + 1, :], (SUBLANES, width)), piece)
        pieces.append(piece)
    return jnp.concatenate(pieces, axis=0)


def _gla_level_masks(rows, heads, levels):
    i = lax.broadcasted_iota(jnp.int32, (rows, heads * rows), 0)
    j = lax.broadcasted_iota(jnp.int32, (rows, heads * rows), 1) & (rows - 1)
    masks = {}
    for s in levels:
        if s == 0:
            masks[s] = i == j
        else:
            sh = int(math.log2(s))
            masks[s] = ((i >> sh) == (j >> sh) + 1) & ((i >> (sh + 1)) == (j >> (sh + 1)))
    return masks


def _gla_intra_scores(q, k, bc, levels, masks, head_masks):
    scores = None
    for s in levels:
        if s == 0:
            qt, kt = q, k
        else:
            ref = _level_reference(bc, s)
            qt = q * jnp.exp(jnp.minimum(bc - ref, 0.0))
            kt = k * jnp.exp(jnp.minimum(ref - bc, 0.0))
        rhs_t = _block_diag_rows(kt.astype(BF16), head_masks)
        sc = jnp.where(masks[s], _dot_nt(qt.astype(BF16), rhs_t), 0.0)
        scores = sc if scores is None else scores + sc
    return scores


def _block_diag_rows(x, col_masks):
    return jnp.concatenate([x * m for m in col_masks], axis=0)


def _head_masks(rows, heads, width):
    lane = lax.broadcasted_iota(jnp.int32, (rows, heads * width), 1)
    sh = int(math.log2(width))
    return [jnp.where((lane >> sh) == h, 1.0, 0.0).astype(BF16) for h in range(heads)]


def _mixer_tail(x, mix_bf16, w_out, g1, npost, npre2, sc2, sh2, wr_hi, wr_lo):
    mix = _dot(mix_bf16, w_out)
    x1 = x + g1 * _rms(mix, npost)
    h2 = _rms(x1, npre2) * (1.0 + sc2) + sh2
    h_hi, h_lo = _split2(h2)
    logits_t = _dot_nt(wr_hi, h_hi) + _dot_nt(wr_hi, h_lo) + _dot_nt(wr_lo, h_hi)
    return x1, h2, logits_t


def _prompt_mixer_kernel(*refs, n_seq, ret_heads, gla_heads):
    @pl.when(pl.program_id(0) < n_seq)
    def _():
        _prompt_mixer_body(*refs, ret_heads=ret_heads, gla_heads=gla_heads)

    @pl.when(pl.program_id(0) >= n_seq)
    def _():
        for out_ref in refs[16:20]:
            out_ref[...] = jnp.zeros_like(out_ref)


def _store_token_tiles(h3_ref, h2):
    rows, d = h2.shape
    nchunk = d // LANES
    for c in range(nchunk):
        h3_ref[pl.ds(c, rows, stride=nchunk), :] = h2[:, c * LANES:(c + 1) * LANES]


def _prompt_mixer_body(x_ref, mod_ref, npre_ref, npost_ref, npre2_ref, gbias_ref, rnw_ref, rnb_ref,
                       gnw_ref, win_ref, gup_ref, wout_ref, wrhi_ref, wrlo_ref, cos_ref, sin_ref,
                       x1_ref, h2_ref, lg_ref, h3_ref, sret_ref, sgla_ref,
                       proj_scr, mix_scr, dmat_scr, sret_scr, sgla_scr, *, ret_heads, gla_heads):
    t = pl.program_id(1)
    tt = x_ref.shape[1]
    rdk = 128
    rdv = 128
    gdk = gup_ref.shape[1] // gla_heads
    gdv = 128
    rq0, rk0, rv0, rg0 = 0, ret_heads * rdk, 2 * ret_heads * rdk, 2 * ret_heads * rdk + ret_heads * rdv
    gq0 = rg0 + ret_heads * rdv
    gk0 = gq0 + gla_heads * gdk
    gv0 = gk0 + gla_heads * gdk
    gg0 = gv0 + gla_heads * gdv
    ga0 = gg0 + gla_heads * gdv
    log_gamma = [math.log(1.0 - 2.0 ** (-5.0 - h)) for h in range(ret_heads)]

    @pl.when(jnp.logical_and(pl.program_id(0) == 0, t == 0))
    def _():
        i = lax.broadcasted_iota(jnp.int32, (tt, tt), 0)
        j = lax.broadcasted_iota(jnp.int32, (tt, tt), 1)
        for h in range(ret_heads):
            dmat_scr[h] = jnp.where(i >= j, jnp.exp(jnp.where(i >= j, (i - j).astype(F32) * log_gamma[h], 0.0)), 0.0)

    @pl.when(t == 0)
    def _():
        sret_scr[...] = jnp.zeros_like(sret_scr)
        sgla_scr[...] = jnp.zeros_like(sgla_scr)

    x = x_ref[0]
    mod = mod_ref[0]
    sh1, sc1, g1, sh2, sc2 = mod[0:1], mod[1:2], mod[2:3], mod[3:4], mod[4:5]
    h = _rms(x, npre_ref[...]) * (1.0 + sc1) + sh1
    proj_scr[...] = _dot(h.astype(BF16), win_ref[...])

    cos = cos_ref[...]
    sin = sin_ref[...]
    row = lax.broadcasted_iota(jnp.int32, (tt, rdk), 0).astype(F32)
    for hd in range(ret_heads):
        lg = log_gamma[hd]
        q = _rotary(proj_scr[:, rq0 + hd * rdk: rq0 + (hd + 1) * rdk], cos, sin)
        k = _rotary(proj_scr[:, rk0 + hd * rdk: rk0 + (hd + 1) * rdk], cos, sin) * (rdk ** -0.5)
        v = proj_scr[:, rv0 + hd * rdv: rv0 + (hd + 1) * rdv].astype(BF16)
        g = proj_scr[:, rg0 + hd * rdv: rg0 + (hd + 1) * rdv]
        qb = q.astype(BF16)
        s_old = sret_scr[hd]
        scores = (_dot_nt(qb, k.astype(BF16)) * dmat_scr[hd]).astype(BF16)
        o = _dot(scores, v) + jnp.exp((row + 1.0) * lg) * _dot(qb, s_old.astype(BF16))
        k_dec = (k * jnp.exp((tt - 1.0 - row) * lg)).astype(BF16)
        sret_scr[hd] = s_old * math.exp(tt * lg) + _dot_tn(k_dec, v)
        mu = jnp.mean(o, axis=-1, keepdims=True)
        oc = o - mu
        var = jnp.mean(oc * oc, axis=-1, keepdims=True)
        y = oc * lax.rsqrt(var + EPS) * rnw_ref[:, hd * rdv:(hd + 1) * rdv] + rnb_ref[:, hd * rdv:(hd + 1) * rdv]
        mix_scr[:, hd * rdv:(hd + 1) * rdv] = (_silu(g) * y).astype(BF16)

    c = GLA_CHUNK
    n_chunks = tt // c
    gw = gla_heads * gdk
    ga = proj_scr[:, ga0:ga0 + LANES].astype(BF16)
    logit = _dot(ga, gup_ref[...]) + gbias_ref[...]
    la = _log_sigmoid(logit) * (1.0 / GLA_GATE_NORM)
    ri = lax.broadcasted_iota(jnp.int32, (tt, tt), 0)
    ci = lax.broadcasted_iota(jnp.int32, (tt, tt), 1)
    csh = int(math.log2(c))
    tril = jnp.where((ri >= ci) & ((ri >> csh) == (ci >> csh)), 1.0, 0.0).astype(BF16)
    la_hi, la_mid, la_lo = _split3(la)
    bcum = _dot(tril, la_hi) + _dot(tril, la_mid) + _dot(tril, la_lo)
    levels = [s for s in (32, 16, 8, 4, 2, 1, 0) if s < c]
    masks = _gla_level_masks(c, gla_heads, levels)
    hm_k = _head_masks(c, gla_heads, gdk)
    hm_v = _head_masks(c, gla_heads, gdv)
    bd_mask = (lax.broadcasted_iota(jnp.int32, (gla_heads * gdv, gw), 0) >> int(math.log2(gdv))) == (
        lax.broadcasted_iota(jnp.int32, (gla_heads * gdv, gw), 1) >> int(math.log2(gdk)))
    for ch in range(n_chunks):
        r0 = ch * c
        q = proj_scr[r0:r0 + c, gq0:gq0 + gw] * (gdk ** -0.5)
        k = proj_scr[r0:r0 + c, gk0:gk0 + gw]
        v = proj_scr[r0:r0 + c, gv0:gv0 + gla_heads * gdv].astype(BF16)
        bc = bcum[r0:r0 + c]
        scores = _gla_intra_scores(q, k, bc, levels, masks, hm_k)
        o = _dot(scores.astype(BF16), _block_diag_rows(v, hm_v))
        st = sgla_scr[...]
        o = o + _dot_nt((q * jnp.exp(bc)).astype(BF16), st.astype(BF16))
        b_last = bc[c - 1:c, :]
        k_dec = (k * jnp.exp(b_last - bc)).astype(BF16)
        sgla_scr[...] = st * jnp.exp(b_last) + jnp.where(bd_mask, _dot_tn(v, k_dec), 0.0)
        gg = proj_scr[r0:r0 + c, gg0:gg0 + gla_heads * gdv]
        for hd in range(gla_heads):
            oh = o[:, hd * gdv:(hd + 1) * gdv]
            y = oh * lax.rsqrt(jnp.mean(oh * oh, axis=-1, keepdims=True) + EPS) * gnw_ref[:, hd * gdv:(hd + 1) * gdv]
            mix_scr[r0:r0 + c, ret_heads * rdv + hd * gdv: ret_heads * rdv + (hd + 1) * gdv] = (
                _silu(gg[:, hd * gdv:(hd + 1) * gdv]) * y).astype(BF16)

    x1, h2, logits_t = _mixer_tail(x, mix_scr[...], wout_ref[...], g1, npost_ref[...], npre2_ref[...],
                                   sc2, sh2, wrhi_ref[...], wrlo_ref[...])
    x1_ref[...] = x1
    h2_ref[...] = h2
    lg_ref[...] = logits_t
    _store_token_tiles(h3_ref, h2)

    @pl.when(t == pl.num_programs(1) - 1)
    def _():
        sret_ref[0] = sret_scr[...]
        sgla_ref[0] = sgla_scr[...]


def _prompt_mixer(x, mod, p, cos, sin, *, tile, n_total):
    b, t, d = x.shape
    nt = t // tile
    tiles = n_total // tile
    extra = -(-(tiles - b * nt) // nt)
    rh, gh = p["ret_heads"], p["gla_heads"]
    in_w = p["w_in"].shape[1]
    ne = p["wr_hi"].shape[0]
    gw = p["gup"].shape[1]
    full = lambda a: pl.BlockSpec(a.shape, lambda i, j: (0,) * a.ndim)
    vecs = [p["npre"], p["npost"], p["npre2"], p["gbias"], p["rnw"], p["rnb"], p["gnw"]]
    mats = [p["w_in"], p["gup"], p["w_out"], p["wr_hi"], p["wr_lo"]]
    kern = functools.partial(_prompt_mixer_kernel, n_seq=b, ret_heads=rh, gla_heads=gh)
    seq = lambda i: jnp.minimum(i, b - 1)
    out_tile = lambda i, j: jnp.minimum(i * nt + j, tiles - 1)
    return pl.pallas_call(
        kern,
        grid=(b + extra, nt),
        in_specs=[pl.BlockSpec((1, tile, d), lambda i, j: (seq(i), j, 0)),
                  pl.BlockSpec((1, 6, d), lambda i, j: (seq(i), 0, 0))]
                 + [full(a) for a in vecs] + [full(a) for a in mats]
                 + [pl.BlockSpec((tile, LANES), lambda i, j: (j, 0)),
                    pl.BlockSpec((tile, LANES), lambda i, j: (j, 0))],
        out_specs=[pl.BlockSpec((tile, d), lambda i, j: (out_tile(i, j), 0)),
                   pl.BlockSpec((tile, d), lambda i, j: (out_tile(i, j), 0)),
                   pl.BlockSpec((ne, tile), lambda i, j: (0, out_tile(i, j))),
                   pl.BlockSpec((tile * (d // LANES), LANES), lambda i, j: (out_tile(i, j), 0)),
                   pl.BlockSpec((1, rh, 128, 128), lambda i, j: (seq(i), 0, 0, 0)),
                   pl.BlockSpec((1, gh * 128, gw), lambda i, j: (seq(i), 0, 0))],
        out_shape=[jax.ShapeDtypeStruct((n_total, d), F32),
                   jax.ShapeDtypeStruct((n_total, d), F32),
                   jax.ShapeDtypeStruct((ne, n_total), F32),
                   jax.ShapeDtypeStruct((n_total * (d // LANES), LANES), F32),
                   jax.ShapeDtypeStruct((b, rh, 128, 128), F32),
                   jax.ShapeDtypeStruct((b, gh * 128, gw), F32)],
        scratch_shapes=[pltpu.VMEM((tile, in_w), F32),
                        pltpu.VMEM((tile, p["w_out"].shape[0]), BF16),
                        pltpu.VMEM((rh, tile, tile), F32),
                        pltpu.VMEM((rh, 128, 128), F32),
                        pltpu.VMEM((gh * 128, gw), F32)],
        compiler_params=pltpu.CompilerParams(
            dimension_semantics=("arbitrary", "arbitrary"), vmem_limit_bytes=VMEM_LIMIT_BYTES),
        name="prompt_mixer",
    )(x, mod, *vecs, *mats, cos, sin)


def _sample_mixer_kernel(x_ref, mod_ref, npre_ref, npost_ref, npre2_ref, gbias_ref, rnw_ref, rnb_ref,
                         gnw_ref, win_ref, gup_ref, wout_ref, wrhi_ref, wrlo_ref, cos_ref, sin_ref,
                         sret_in_ref, sgla_in_ref, x1_any, h2_any, lg_any, h3_any,
                         x1_ref, h2_ref, lg_ref, h3_ref, sret_ref, sgla_ref,
                         proj_scr, mix_scr, qrot_scr, krot_scr, oret_scr, bc_scr, gqe_scr, ogla_scr,
                         *, ret_heads, gla_heads, ts):
    del x1_any, h2_any, lg_any, h3_any
    rows = x_ref.shape[0]
    nb = rows // ts
    rdk = 128
    rdv = 128
    gdk = gup_ref.shape[1] // gla_heads
    gdv = 128
    gw = gla_heads * gdk
    gvw = gla_heads * gdv
    rq0, rk0, rv0, rg0 = 0, ret_heads * rdk, 2 * ret_heads * rdk, 2 * ret_heads * rdk + ret_heads * rdv
    gq0 = rg0 + ret_heads * rdv
    gk0 = gq0 + gw
    gv0 = gk0 + gw
    gg0 = gv0 + gvw
    ga0 = gg0 + gvw
    log_gamma = [math.log(1.0 - 2.0 ** (-5.0 - h)) for h in range(ret_heads)]
    tsh = int(math.log2(ts))

    def per_row(m):
        return jnp.broadcast_to(m[:, None, :], (nb, ts, m.shape[-1])).reshape(rows, m.shape[-1])

    x = x_ref[...]
    sh1, sc1, g1, sh2, sc2 = (per_row(mod_ref[i]) for i in range(5))
    h = _rms(x, npre_ref[...]) * (1.0 + sc1) + sh1
    proj_scr[...] = _dot(h.astype(BF16), win_ref[...])

    cos = cos_ref[...]
    sin = sin_ref[...]
    ri = lax.broadcasted_iota(jnp.int32, (rows, rows), 0)
    ci = lax.broadcasted_iota(jnp.int32, (rows, rows), 1)
    same = (ri >= ci) & ((ri >> tsh) == (ci >> tsh))
    for hd in range(ret_heads):
        lg = log_gamma[hd]
        q = _rotary(proj_scr[:, rq0 + hd * rdk: rq0 + (hd + 1) * rdk], cos, sin)
        k = _rotary(proj_scr[:, rk0 + hd * rdk: rk0 + (hd + 1) * rdk], cos, sin) * (rdk ** -0.5)
        v = proj_scr[:, rv0 + hd * rdv: rv0 + (hd + 1) * rdv].astype(BF16)
        qrot_scr[:, hd * rdk:(hd + 1) * rdk] = q
        krot_scr[:, hd * rdk:(hd + 1) * rdk] = k
        dmat = jnp.where(same, jnp.exp(jnp.where(same, (ri - ci).astype(F32) * lg, 0.0)), 0.0)
        scores = (_dot_nt(q.astype(BF16), k.astype(BF16)) * dmat).astype(BF16)
        oret_scr[:, hd * rdv:(hd + 1) * rdv] = _dot(scores, v)

    ga = proj_scr[:, ga0:ga0 + LANES].astype(BF16)
    logit = _dot(ga, gup_ref[...]) + gbias_ref[...]
    la = _log_sigmoid(logit) * (1.0 / GLA_GATE_NORM)
    tril = jnp.where(same, 1.0, 0.0).astype(BF16)
    la_hi, la_mid, la_lo = _split3(la)
    bcum = _dot(tril, la_hi) + _dot(tril, la_mid) + _dot(tril, la_lo)
    bc_scr[...] = bcum
    c = min(GLA_CHUNK, rows)
    levels = [s for s in (32, 16, 8, 4, 2, 1, 0) if s < ts]
    masks = _gla_level_masks(c, gla_heads, levels)
    hm_k = _head_masks(c, gla_heads, gdk)
    hm_v = _head_masks(c, gla_heads, gdv)
    for ch in range(rows // c):
        r0 = ch * c
        q = proj_scr[r0:r0 + c, gq0:gq0 + gw] * (gdk ** -0.5)
        k = proj_scr[r0:r0 + c, gk0:gk0 + gw]
        v = proj_scr[r0:r0 + c, gv0:gv0 + gvw].astype(BF16)
        bc = bcum[r0:r0 + c]
        scores = _gla_intra_scores(q, k, bc, levels, masks, hm_k)
        ogla_scr[r0:r0 + c, :] = _dot(scores.astype(BF16), _block_diag_rows(v, hm_v))
        gqe_scr[r0:r0 + c, :] = q * jnp.exp(bc)

    trow = lax.broadcasted_iota(jnp.int32, (ts, rdk), 0).astype(F32)
    eye = lax.broadcasted_iota(jnp.int32, (gw, gw), 0) == lax.broadcasted_iota(jnp.int32, (gw, gw), 1)
    zero_blk = jnp.zeros((gdk, gdv), BF16)

    def element(b, carry):
        rs = pl.ds(pl.multiple_of(b * ts, ts), ts)
        for hd in range(ret_heads):
            lg = log_gamma[hd]
            q = qrot_scr[rs, hd * rdk:(hd + 1) * rdk]
            k = krot_scr[rs, hd * rdk:(hd + 1) * rdk]
            v = proj_scr[rs, rv0 + hd * rdv: rv0 + (hd + 1) * rdv]
            s_old = sret_in_ref[b, hd]
            oret_scr[rs, hd * rdv:(hd + 1) * rdv] += jnp.exp((trow + 1.0) * lg) * _dot(
                q.astype(BF16), s_old.astype(BF16))
            k_dec = (k * jnp.exp((ts - 1.0 - trow) * lg)).astype(BF16)
            sret_ref[b, hd] = s_old * math.exp(ts * lg) + _dot_tn(k_dec, v.astype(BF16))
        s_b = sgla_in_ref[b]
        s_bd = jnp.concatenate(
            [jnp.concatenate([s_b[hd].astype(BF16) if h2 == hd else zero_blk for h2 in range(gla_heads)], axis=1)
             for hd in range(gla_heads)], axis=0)
        ogla_scr[rs, :] += _dot(gqe_scr[rs, :].astype(BF16), s_bd)
        bc = bc_scr[rs, :]
        b_last = bc[ts - 1:ts, :]
        k_dec = (proj_scr[rs, gk0:gk0 + gw] * jnp.exp(b_last - bc)).astype(BF16)
        upd = _dot_tn(k_dec, proj_scr[rs, gv0:gv0 + gvw].astype(BF16))
        decay_col = jnp.sum(jnp.where(eye, jnp.broadcast_to(jnp.exp(b_last), (gw, gw)), 0.0), axis=1, keepdims=True)
        for hd in range(gla_heads):
            sgla_ref[b, hd] = (s_b[hd] * decay_col[hd * gdk:(hd + 1) * gdk]
                               + upd[hd * gdk:(hd + 1) * gdk, hd * gdv:(hd + 1) * gdv])
        return carry

    lax.fori_loop(0, nb, element, 0)

    for hd in range(ret_heads):
        o = oret_scr[:, hd * rdv:(hd + 1) * rdv]
        g = proj_scr[:, rg0 + hd * rdv: rg0 + (hd + 1) * rdv]
        mu = jnp.mean(o, axis=-1, keepdims=True)
        oc = o - mu
        var = jnp.mean(oc * oc, axis=-1, keepdims=True)
        y = oc * lax.rsqrt(var + EPS) * rnw_ref[:, hd * rdv:(hd + 1) * rdv] + rnb_ref[:, hd * rdv:(hd + 1) * rdv]
        mix_scr[:, hd * rdv:(hd + 1) * rdv] = (_silu(g) * y).astype(BF16)
    for hd in range(gla_heads):
        oh = ogla_scr[:, hd * gdv:(hd + 1) * gdv]
        gg = proj_scr[:, gg0 + hd * gdv: gg0 + (hd + 1) * gdv]
        y = oh * lax.rsqrt(jnp.mean(oh * oh, axis=-1, keepdims=True) + EPS) * gnw_ref[:, hd * gdv:(hd + 1) * gdv]
        mix_scr[:, ret_heads * rdv + hd * gdv: ret_heads * rdv + (hd + 1) * gdv] = (_silu(gg) * y).astype(BF16)

    x1, h2, logits_t = _mixer_tail(x, mix_scr[...], wout_ref[...], g1, npost_ref[...], npre2_ref[...],
                                   sc2, sh2, wrhi_ref[...], wrlo_ref[...])
    x1_ref[...] = x1
    h2_ref[...] = h2
    lg_ref[...] = logits_t
    _store_token_tiles(h3_ref, h2)


def _sample_mixer(x, mod_t, p, cos, sin, state_ret, state_gla, x1_buf, h2_buf, lg_buf, h3_buf, *, ts, tile):
    n, d = x.shape
    off = (x1_buf.shape[0] - n) // tile
    nb = tile // ts
    rh, gh = p["ret_heads"], p["gla_heads"]
    in_w = p["w_in"].shape[1]
    ne = p["wr_hi"].shape[0]
    gw = p["gup"].shape[1]
    gdk = gw // gh
    full = lambda a: pl.BlockSpec(a.shape, lambda i: (0,) * a.ndim)
    vecs = [p["npre"], p["npost"], p["npre2"], p["gbias"], p["rnw"], p["rnb"], p["gnw"]]
    mats = [p["w_in"], p["gup"], p["w_out"], p["wr_hi"], p["wr_lo"]]
    kern = functools.partial(_sample_mixer_kernel, ret_heads=rh, gla_heads=gh, ts=ts)
    n_in = 2 + len(vecs) + len(mats) + 4 + 4
    return pl.pallas_call(
        kern,
        grid=(n // tile,),
        in_specs=[pl.BlockSpec((tile, d), lambda i: (i, 0)),
                  pl.BlockSpec((6, nb, d), lambda i: (0, i, 0))]
                 + [full(a) for a in vecs] + [full(a) for a in mats]
                 + [full(cos), full(sin),
                    pl.BlockSpec((nb, rh, 128, 128), lambda i: (i, 0, 0, 0)),
                    pl.BlockSpec((nb, gh, gdk, 128), lambda i: (i, 0, 0, 0)),
                    pl.BlockSpec(memory_space=pl.ANY), pl.BlockSpec(memory_space=pl.ANY),
                    pl.BlockSpec(memory_space=pl.ANY), pl.BlockSpec(memory_space=pl.ANY)],
        out_specs=[pl.BlockSpec((tile, d), lambda i: (off + i, 0)),
                   pl.BlockSpec((tile, d), lambda i: (off + i, 0)),
                   pl.BlockSpec((ne, tile), lambda i: (0, off + i)),
                   pl.BlockSpec((tile * (d // LANES), LANES), lambda i: (off + i, 0)),
                   pl.BlockSpec((nb, rh, 128, 128), lambda i: (i, 0, 0, 0)),
                   pl.BlockSpec((nb, gh, gdk, 128), lambda i: (i, 0, 0, 0))],
        out_shape=[jax.ShapeDtypeStruct(x1_buf.shape, F32),
                   jax.ShapeDtypeStruct(h2_buf.shape, F32),
                   jax.ShapeDtypeStruct(lg_buf.shape, F32),
                   jax.ShapeDtypeStruct(h3_buf.shape, F32),
                   jax.ShapeDtypeStruct(state_ret.shape, F32),
                   jax.ShapeDtypeStruct(state_gla.shape, F32)],
        input_output_aliases={n_in - 4: 0, n_in - 3: 1, n_in - 2: 2, n_in - 1: 3},
        scratch_shapes=[pltpu.VMEM((tile, in_w), F32),
                        pltpu.VMEM((tile, p["w_out"].shape[0]), BF16),
                        pltpu.VMEM((tile, rh * 128), F32),
                        pltpu.VMEM((tile, rh * 128), F32),
                        pltpu.VMEM((tile, rh * 128), F32),
                        pltpu.VMEM((tile, gw), F32),
                        pltpu.VMEM((tile, gw), F32),
                        pltpu.VMEM((tile, gh * 128), F32)],
        compiler_params=pltpu.CompilerParams(
            dimension_semantics=("arbitrary",), vmem_limit_bytes=VMEM_LIMIT_BYTES),
        name="sample_mixer",
    )(x, mod_t, *vecs, *mats, cos, sin, state_ret, state_gla, x1_buf, h2_buf, lg_buf, h3_buf)


def _router_kernel(lg_ref, bias_ref, idx_ref, w_ref, cnt_ref, cnt_scr, *, n_groups, topk_groups, top_k):
    ne, tn = lg_ref.shape
    gsz = ne // n_groups
    neg = -jnp.inf

    @pl.when(pl.program_id(1) == 0)
    def _():
        cnt_scr[...] = jnp.zeros_like(cnt_scr)

    scores = jax.nn.sigmoid(lg_ref[...])
    sel = scores + bias_ref[...][:, 0:1]
    sel3 = sel.reshape(n_groups, gsz, tn)
    mem = lax.broadcasted_iota(jnp.int32, (n_groups, gsz, tn), 1)
    m1 = jnp.max(sel3, axis=1, keepdims=True)
    first = jnp.min(jnp.where(sel3 == m1, mem, gsz), axis=1, keepdims=True)
    m2 = jnp.max(jnp.where(mem == first, neg, sel3), axis=1, keepdims=True)
    gscore = (m1 + m2).reshape(n_groups, tn)
    gi = lax.broadcasted_iota(jnp.int32, (n_groups, tn), 0)
    gsel = jnp.zeros((n_groups, tn), jnp.bool_)
    work = gscore
    for _ in range(topk_groups):
        mx = jnp.max(work, axis=0, keepdims=True)
        pick = gi == jnp.min(jnp.where(work == mx, gi, n_groups), axis=0, keepdims=True)
        gsel = jnp.logical_or(gsel, pick)
        work = jnp.where(pick, neg, work)
    emask = jnp.broadcast_to(gsel[:, None, :], (n_groups, gsz, tn)).reshape(ne, tn)
    ei = lax.broadcasted_iota(jnp.int32, (ne, tn), 0)
    work = jnp.where(emask, sel, neg)
    chosen_any = jnp.zeros((ne, tn), jnp.bool_)
    idx_rows, w_rows = [], []
    for _ in range(top_k):
        mx = jnp.max(work, axis=0, keepdims=True)
        first_e = jnp.min(jnp.where(work == mx, ei, ne), axis=0, keepdims=True)
        pick = ei == first_e
        chosen_any = jnp.logical_or(chosen_any, pick)
        work = jnp.where(pick, neg, work)
        idx_rows.append(first_e)
        w_rows.append(jnp.sum(jnp.where(pick, scores, 0.0), axis=0, keepdims=True))
    wsum = w_rows[0]
    for r in w_rows[1:]:
        wsum = wsum + r
    idx_ref[...] = jnp.concatenate(idx_rows, axis=0)
    w_ref[...] = jnp.concatenate(w_rows, axis=0) / wsum * ROUTED_SCALE
    cnt_scr[...] += _dot(jnp.where(chosen_any, 1.0, 0.0).astype(BF16), jnp.ones((tn, LANES), BF16))
    cnt_ref[0] = cnt_scr[...]


def _router(logits_t, bias, *, parts, tile):
    ne, n = logits_t.shape
    tiles = n // parts // tile
    kern = functools.partial(_router_kernel, n_groups=N_GROUPS, topk_groups=TOPK_GROUPS, top_k=TOP_K)
    return pl.pallas_call(
        kern,
        grid=(parts, tiles),
        in_specs=[pl.BlockSpec((ne, tile), lambda p, j: (0, p * tiles + j)),
                  pl.BlockSpec((ne, LANES), lambda p, j: (0, 0))],
        out_specs=[pl.BlockSpec((TOP_K, tile), lambda p, j: (0, p * tiles + j)),
                   pl.BlockSpec((TOP_K, tile), lambda p, j: (0, p * tiles + j)),
                   pl.BlockSpec((1, ne, LANES), lambda p, j: (p, 0, 0))],
        out_shape=[jax.ShapeDtypeStruct((TOP_K, n), jnp.int32),
                   jax.ShapeDtypeStruct((TOP_K, n), F32),
                   jax.ShapeDtypeStruct((parts, ne, LANES), F32)],
        scratch_shapes=[pltpu.VMEM((ne, LANES), F32)],
        compiler_params=pltpu.CompilerParams(
            dimension_semantics=("arbitrary", "arbitrary"), vmem_limit_bytes=VMEM_LIMIT_BYTES),
        name="router_topk",
    )(logits_t, jnp.broadcast_to(bias.reshape(ne, 1), (ne, LANES)))


def _dispatch_plan(idx_t, w_t, counts, *, parts, blk, rows_per_token):
    k, n = idx_t.shape
    npart = n // parts
    ne = counts.shape[1]
    stride = npart + blk
    big = ne * stride
    local = jnp.arange(n, dtype=jnp.int32) % npart
    to_parts = lambda a: a.reshape(k, parts, npart).transpose(1, 0, 2).reshape(parts, k * npart)
    keys = to_parts(idx_t * stride + local[None, :])
    wts = to_parts(w_t)
    need = (-counts) % blk
    j = jnp.arange(blk - 1, dtype=jnp.int32)
    pad_keys = jnp.where(j[None, None, :] < need[:, :, None],
                         jnp.arange(ne, dtype=jnp.int32)[None, :, None] * stride + npart + j[None, None, :], big)
    total = k * npart + ne * (blk - 1)
    nblk = -(-total // blk)
    fill = nblk * blk - total
    all_keys = jnp.concatenate([keys, pad_keys.reshape(parts, -1), jnp.full((parts, fill), big, jnp.int32)], axis=1)
    all_w = jnp.concatenate([wts, jnp.zeros((parts, nblk * blk - k * npart), F32)], axis=1)
    sk, sw = lax.sort((all_keys, all_w), dimension=1, num_keys=1)
    valid = sk < big
    slot_tok = sk % stride
    real = jnp.logical_and(valid, slot_tok < npart)
    acc_row = jnp.where(real, slot_tok, npart) * rows_per_token
    src_row = jnp.where(real, slot_tok, 0) * rows_per_token
    w = jnp.where(real, sw, 0.0)
    blocks_per_expert = (counts + blk - 1) // blk
    first_blk = jnp.concatenate([jnp.zeros((parts, 1), jnp.int32),
                                 jnp.cumsum(blocks_per_expert, axis=1, dtype=jnp.int32)], axis=1)
    return src_row.reshape(-1), acc_row.reshape(-1), w.reshape(-1), first_blk, nblk


def _moe_kernel(first_ref, src_hbm, dst_hbm, wts_hbm, h3_ref, h2_ref, x1_ref, g2_ref, npost_ref,
                wg_ref, wu_ref, wd_ref, wsg_ref, wsu_ref, wsd_ref, ya_ref, yb_ref,
                acc_scr, g_scr, y2_scr, xs_scr, wgu_scr, wdn_scr, src_smem, dst_smem, wts_smem, sem,
                *, ne, nblk, blk, ts, n_epi, tiles_a):
    p = pl.program_id(0)
    b = pl.program_id(1)
    d = h2_ref.shape[1]
    ff = wg_ref.shape[2]
    nchunk = d // LANES
    tile_at = lambda row: pl.ds(pl.multiple_of(row, nchunk), nchunk)

    depth = src_smem.shape[0]
    n_valid = first_ref[p, ne]

    def table_copies(f):
        slot = f & (depth - 1)
        row = pl.multiple_of((p * nblk + f) * blk, blk)
        return tuple(pltpu.make_async_copy(hbm.at[pl.ds(row, blk)], smem.at[slot], sem.at[i, slot])
                     for i, (hbm, smem) in enumerate(((src_hbm, src_smem), (dst_hbm, dst_smem), (wts_hbm, wts_smem))))

    @pl.when(b == 0)
    def _():
        acc_scr[...] = jnp.zeros_like(acc_scr)
        for f in range(depth - 1):
            @pl.when(f < n_valid)
            def _():
                for cp in table_copies(f):
                    cp.start()

    def expert_block(f, carry):
        slot = f & (depth - 1)
        for cp in table_copies(f):
            cp.wait()

        @pl.when(f + (depth - 1) < n_valid)
        def _():
            for cp in table_copies(f + (depth - 1)):
                cp.start()

        for r in range(blk):
            g_scr[r * nchunk:(r + 1) * nchunk, :] = h3_ref[tile_at(src_smem[slot, r]), :]
        for c in range(nchunk):
            xs_scr[:, c * LANES:(c + 1) * LANES] = g_scr[pl.ds(c, blk, stride=nchunk), :].astype(BF16)
        gu = _dot(xs_scr[...], wgu_scr[...])
        hid = (_silu(gu[:, :ff]) * gu[:, ff:]).astype(BF16)
        y = _dot(hid, wdn_scr[...])
        for c in range(nchunk):
            y2_scr[pl.ds(c, blk, stride=nchunk), :] = y[:, c * LANES:(c + 1) * LANES]
        for g in range(blk // SUBLANES):
            rows = [g * SUBLANES + i for i in range(SUBLANES)]
            dsts = [dst_smem[slot, r] for r in rows]
            new = [acc_scr[tile_at(t), :] + wts_smem[slot, r] * y2_scr[r * nchunk:(r + 1) * nchunk, :]
                   for t, r in zip(dsts, rows)]
            for t, v in zip(dsts, new):
                acc_scr[tile_at(t), :] = v
        return carry

    @pl.when(b < ne)
    def _():
        e = jnp.minimum(b, ne - 1)
        f0, f1 = first_ref[p, e], first_ref[p, e + 1]

        @pl.when(f1 > f0)
        def _():
            wgu_scr[:, :ff] = wg_ref[0].astype(BF16)
            wgu_scr[:, ff:] = wu_ref[0].astype(BF16)
            wdn_scr[...] = wd_ref[0].astype(BF16)
            lax.fori_loop(f0, f1, expert_block, 0)

    @pl.when(b >= ne)
    def _():
        rows = ya_ref.shape[0]
        t0 = pl.multiple_of((b - ne) * rows, rows)
        routed = jnp.concatenate(
            [acc_scr[pl.ds(t0 * nchunk + c, rows, stride=nchunk), :] for c in range(nchunk)], axis=1)
        hs = h2_ref[...].astype(BF16)
        hid = (_silu(_dot(hs, wsg_ref[...])) * _dot(hs, wsu_ref[...])).astype(BF16)
        f = routed + _dot(hid, wsd_ref[...])
        g2 = g2_ref[0]
        g2 = jnp.broadcast_to(g2[:, None, :], (rows // ts, ts, d)).reshape(rows, d)
        y = x1_ref[...] + g2 * _rms(f, npost_ref[...])
        tile = p * n_epi + (b - ne)

        @pl.when(tile < tiles_a)
        def _():
            ya_ref[...] = y

        @pl.when(tile >= tiles_a)
        def _():
            yb_ref[...] = y


def _moe(h2, h3, x1, g2_tiles, npost, plan, w_gate, w_up, w_down, wsg, wsu, wsd, *, parts, blk, ts, n_a):
    src_row, acc_row, wts, first_blk, nblk = plan
    n, d = h2.shape
    npart = n // parts
    ne, _, ff = w_gate.shape
    epi = LANES
    n_epi = npart // epi
    tiles_a = n_a // epi
    tiles_b = (n - n_a) // epi
    nchunk = d // LANES
    kern = functools.partial(_moe_kernel, ne=ne, nblk=nblk, blk=blk, ts=ts, n_epi=n_epi, tiles_a=tiles_a)
    epi_of = lambda p, b: p * n_epi + jnp.clip(b - ne, 0, n_epi - 1)
    exp_of = lambda p, b: jnp.minimum(b, ne - 1)
    full = lambda a: pl.BlockSpec(a.shape, lambda p, b, fb: (0,) * a.ndim)
    grid_spec = pltpu.PrefetchScalarGridSpec(
        num_scalar_prefetch=1,
        grid=(parts, ne + n_epi),
        in_specs=[
            pl.BlockSpec(memory_space=pl.ANY),
            pl.BlockSpec(memory_space=pl.ANY),
            pl.BlockSpec(memory_space=pl.ANY),
            pl.BlockSpec((npart * nchunk, LANES), lambda p, b, fb: (p, 0), pipeline_mode=pl.Buffered(1)),
            pl.BlockSpec((epi, d), lambda p, b, fb: (epi_of(p, b), 0)),
            pl.BlockSpec((epi, d), lambda p, b, fb: (epi_of(p, b), 0)),
            pl.BlockSpec((1, epi // ts, d), lambda p, b, fb: (epi_of(p, b), 0, 0)),
            full(npost),
            pl.BlockSpec((1, d, ff), lambda p, b, fb: (exp_of(p, b), 0, 0)),
            pl.BlockSpec((1, d, ff), lambda p, b, fb: (exp_of(p, b), 0, 0)),
            pl.BlockSpec((1, ff, d), lambda p, b, fb: (exp_of(p, b), 0, 0)),
            full(wsg), full(wsu), full(wsd),
        ],
        out_specs=[
            pl.BlockSpec((epi, d), lambda p, b, fb: (jnp.minimum(epi_of(p, b), tiles_a - 1), 0)),
            pl.BlockSpec((epi, d), lambda p, b, fb: (jnp.clip(epi_of(p, b) - tiles_a, 0, tiles_b - 1), 0))],
        scratch_shapes=[pltpu.VMEM(((npart + 1) * (d // LANES), LANES), F32),
                        pltpu.VMEM((blk * (d // LANES), LANES), F32),
                        pltpu.VMEM((blk * (d // LANES), LANES), F32),
                        pltpu.VMEM((blk, d), BF16),
                        pltpu.VMEM((d, 2 * ff), BF16),
                        pltpu.VMEM((ff, d), BF16),
                        pltpu.SMEM((MOE_TABLE_DEPTH, blk), jnp.int32),
                        pltpu.SMEM((MOE_TABLE_DEPTH, blk), jnp.int32),
                        pltpu.SMEM((MOE_TABLE_DEPTH, blk), F32),
                        pltpu.SemaphoreType.DMA((3, MOE_TABLE_DEPTH))],
    )
    return pl.pallas_call(
        kern,
        grid_spec=grid_spec,
        out_shape=[jax.ShapeDtypeStruct((n_a, d), F32), jax.ShapeDtypeStruct((n - n_a, d), F32)],
        compiler_params=pltpu.CompilerParams(
            dimension_semantics=("arbitrary", "arbitrary"), vmem_limit_bytes=VMEM_LIMIT_BYTES),
        name="moe_experts",
    )(first_blk, src_row, acc_row, wts, h3, h2, x1, g2_tiles, npost, w_gate, w_up, w_down, wsg, wsu, wsd)


def _rope_tables(pos):
    half = 64
    inv_freq = ROPE_BASE ** (-jnp.arange(half, dtype=F32) / half)
    ang = pos[:, None] * inv_freq[None, :]
    cos, sin = jnp.cos(ang), jnp.sin(ang)
    return jnp.concatenate([cos, cos], axis=1), jnp.concatenate([-sin, sin], axis=1)


def _mixer_params(norm_pre_mix, norm_post_mix, norm_pre_ffn, w_in, gla_gate_up, gla_gate_bias,
                  ret_norm_w, ret_norm_b, gla_norm_w, w_out, w_router):
    d, in_w = w_in.shape
    pad = (-in_w) % LANES
    rank = gla_gate_up.shape[0]
    wr_hi, wr_lo = _split2(w_router.T)
    return dict(
        ret_heads=ret_norm_w.shape[0], gla_heads=gla_norm_w.shape[0],
        npre=norm_pre_mix.reshape(1, d), npost=norm_post_mix.reshape(1, d), npre2=norm_pre_ffn.reshape(1, d),
        gbias=gla_gate_bias.reshape(1, -1), rnw=ret_norm_w.reshape(1, -1), rnb=ret_norm_b.reshape(1, -1),
        gnw=gla_norm_w.reshape(1, -1),
        w_in=jnp.pad(w_in, ((0, 0), (0, pad))).astype(BF16),
        gup=jnp.pad(gla_gate_up, ((0, LANES - rank), (0, 0))).astype(BF16),
        w_out=w_out.astype(BF16), wr_hi=wr_hi, wr_lo=wr_lo)


def kernel(x_prompt, x_sample, state_ret, state_gla, c_prompt, c_sample, w_ada, b_ada, norm_pre_mix, norm_post_mix, norm_pre_ffn, norm_post_ffn, w_in, gla_gate_up, gla_gate_bias, ret_norm_w, ret_norm_b, gla_norm_w, w_out, w_router, router_bias, w_exp_gate, w_exp_up, w_exp_down, w_sh_gate, w_sh_up, w_sh_down):
    bp, tp, d = x_prompt.shape
    bs, ts, _ = x_sample.shape
    l = 0
    c_all = jnp.concatenate([c_prompt, c_sample], axis=0)
    mod = _ada(c_all, w_ada[l], b_ada[l]).reshape(bp + bs, 6, d)
    p = _mixer_params(norm_pre_mix[l], norm_post_mix[l], norm_pre_ffn[l], w_in[l], gla_gate_up[l],
                      gla_gate_bias[l], ret_norm_w[l], ret_norm_b[l], gla_norm_w[l], w_out[l], w_router[l])
    n_p, n_s = bp * tp, bs * ts
    n = n_p + n_s
    cos_p, sin_p = _rope_tables(jnp.arange(tp, dtype=F32))
    x1, h2, logits_t, h3, sret_p, sgla_p = _prompt_mixer(x_prompt, mod[:bp], p, cos_p, sin_p,
                                                         tile=PROMPT_TILE, n_total=n)
    cos_s, sin_s = _rope_tables(PAST_LEN + jnp.arange(ts, dtype=F32))
    cos_s, sin_s = jnp.tile(cos_s, (SAMPLE_TILE // ts, 1)), jnp.tile(sin_s, (SAMPLE_TILE // ts, 1))
    x1, h2, logits_t, h3, sret_s, sgla_s = _sample_mixer(
        x_sample.reshape(n_s, d), mod[bp:].transpose(1, 0, 2), p, cos_s, sin_s, state_ret[l], state_gla[l],
        x1, h2, logits_t, h3, ts=ts, tile=SAMPLE_TILE)
    gh, gdk = p["gla_heads"], p["gup"].shape[1] // p["gla_heads"]
    sgla_blocks = sgla_p.reshape(bp, gh, 128, gh, gdk)
    new_gla_p = jnp.stack([sgla_blocks[:, h, :, h, :] for h in range(gh)], axis=1).transpose(0, 1, 3, 2)

    idx_t, w_t, cnt = _router(logits_t, router_bias[l], parts=MOE_PARTS, tile=ROUTER_TILE)
    plan = _dispatch_plan(idx_t, w_t, cnt[:, :, 0].astype(jnp.int32), parts=MOE_PARTS, blk=MOE_BLOCK,
                          rows_per_token=d // LANES)
    per_tile = LANES // ts
    g2 = mod[:, 5, :]
    g2_tiles = jnp.concatenate(
        [jnp.broadcast_to(jnp.repeat(g2[:bp], tp // LANES, axis=0)[:, None, :], (n_p // LANES, per_tile, d)),
         g2[bp:].reshape(n_s // LANES, per_tile, d)], axis=0)
    y_p, y_s = _moe(h2, h3, x1, g2_tiles, norm_post_ffn[l].reshape(1, d), plan,
                    w_exp_gate[l], w_exp_up[l], w_exp_down[l],
                    w_sh_gate[l].astype(BF16), w_sh_up[l].astype(BF16), w_sh_down[l].astype(BF16),
                    parts=MOE_PARTS, blk=MOE_BLOCK, ts=ts, n_a=n_p)
    return (y_p.reshape(bp, tp, d), y_s.reshape(bs, ts, d), sret_p[None], new_gla_p[None],
            sret_s[None], sgla_s[None])
```

```python
import functools
import math

import jax
import jax.numpy as jnp
from jax import lax
from jax.experimental import pallas as pl
from jax.experimental.pallas import tpu as pltpu

F32 = jnp.float32
BF16 = jnp.bfloat16

PAST_LEN = 16384
ROPE_BASE = 10000.0
GLA_GATE_NORM = 16.0
TOP_K = 8
N_GROUPS = 8
TOPK_GROUPS = 4
ROUTED_SCALE = 2.5
EPS = 1e-6

LANES = 128
SUBLANES = 8
VMEM_LIMIT_BYTES = 56 * 1024 * 1024

GLA_CHUNK = 64
PROMPT_TILE = 256
SAMPLE_TILE = 128
ROUTER_TILE = 256
MOE_PARTS = 4
MOE_BLOCK = 256
MOE_TABLE_DEPTH = 8


def _dot(a, b):
    return jnp.dot(a, b, preferred_element_type=F32)


def _dot_nt(a, b):
    return lax.dot_general(a, b, (((1,), (1,)), ((), ())), preferred_element_type=F32)


def _dot_tn(a, b):
    return lax.dot_general(a, b, (((0,), (0,)), ((), ())), preferred_element_type=F32)


def _split2(x):
    hi = x.astype(BF16)
    lo = (x - hi.astype(F32)).astype(BF16)
    return hi, lo


def _split3(x):
    hi = x.astype(BF16)
    r = x - hi.astype(F32)
    mid = r.astype(BF16)
    lo = (r - mid.astype(F32)).astype(BF16)
    return hi, mid, lo


def _rms(x, g):
    return x * lax.rsqrt(jnp.mean(x * x, axis=-1, keepdims=True) + EPS) * g


def _silu(x):
    return x * jax.nn.sigmoid(x)


def _log_sigmoid(x):
    return jnp.minimum(x, 0.0) - jnp.log(1.0 + jnp.exp(-jnp.abs(x)))


def _ada_kernel(c_ref, w_ref, b_ref, o_ref):
    a_hi, a_lo = _split2(_silu(c_ref[...]))
    w_hi, w_lo = _split2(w_ref[...])
    o_ref[...] = _dot(a_hi, w_hi) + _dot(a_hi, w_lo) + _dot(a_lo, w_hi) + b_ref[...]


def _ada(c, w, b):
    rows, d = c.shape
    n = w.shape[1]
    tn = 1024
    return pl.pallas_call(
        _ada_kernel,
        grid=(n // tn,),
        in_specs=[
            pl.BlockSpec((rows, d), lambda j: (0, 0)),
            pl.BlockSpec((d, tn), lambda j: (0, j)),
            pl.BlockSpec((1, tn), lambda j: (0, j)),
        ],
        out_specs=pl.BlockSpec((rows, tn), lambda j: (0, j)),
        out_shape=jax.ShapeDtypeStruct((rows, n), F32),
        compiler_params=pltpu.CompilerParams(
            dimension_semantics=("arbitrary",), vmem_limit_bytes=VMEM_LIMIT_BYTES),
        name="ada_mod",
    )(c, w, b.reshape(1, n))


def _rotary(x, cos, sin_signed):
    return x * cos + pltpu.roll(x, x.shape[-1] // 2, axis=1) * sin_signed


def _level_reference(bc, s):
    rows, width = bc.shape
    pieces = []
    sub = lax.broadcasted_iota(jnp.int32, (SUBLANES, width), 0)
    for g in range(rows // SUBLANES):
        base = g * SUBLANES
        bounds = sorted({((base + r) // (2 * s)) * (2 * s) + s - 1 for r in range(SUBLANES)})
        piece = jnp.broadcast_to(bc[bounds[-1]:bounds[-1] + 1, :], (SUBLANES, width))
        for bm in reversed(bounds[:-1]):
            last_row_of_pair = bm + s - base
            piece = jnp.where(sub <= last_row_of_pair,
                              jnp.broadcast_to(bc[bm:bm + 1, :], (SUBLANES, width)), piece)
        pieces.append(piece)
    return jnp.concatenate(pieces, axis=0)


def _gla_level_masks(rows, heads, levels):
    i = lax.broadcasted_iota(jnp.int32, (rows, heads * rows), 0)
    j = lax.broadcasted_iota(jnp.int32, (rows, heads * rows), 1) & (rows - 1)
    masks = {}
    for s in levels:
        if s == 0:
            masks[s] = i == j
        else:
            sh = int(math.log2(s))
            masks[s] = ((i >> sh) == (j >> sh) + 1) & ((i >> (sh + 1)) == (j >> (sh + 1)))
    return masks


def _gla_intra_scores(q, k, bc, levels, masks, head_masks):
    scores = None
    for s in levels:
        if s == 0:
            qt, kt = q, k
        else:
            ref = _level_reference(bc, s)
            qt = q * jnp.exp(jnp.minimum(bc - ref, 0.0))
            kt = k * jnp.exp(jnp.minimum(ref - bc, 0.0))
        rhs_t = _block_diag_rows(kt.astype(BF16), head_masks)
        sc = jnp.where(masks[s], _dot_nt(qt.astype(BF16), rhs_t), 0.0)
        scores = sc if scores is None else scores + sc
    return scores


def _block_diag_rows(x, col_masks):
    return jnp.concatenate([x * m for m in col_masks], axis=0)


def _head_masks(rows, heads, width):
    lane = lax.broadcasted_iota(jnp.int32, (rows, heads * width), 1)
    sh = int(math.log2(width))
    return [jnp.where((lane >> sh) == h, 1.0, 0.0).astype(BF16) for h in range(heads)]


def _mixer_tail(x, mix_bf16, w_out, g1, npost, npre2, sc2, sh2, wr_hi, wr_lo):
    mix = _dot(mix_bf16, w_out)
    x1 = x + g1 * _rms(mix, npost)
    h2 = _rms(x1, npre2) * (1.0 + sc2) + sh2
    h_hi, h_lo = _split2(h2)
    logits_t = _dot_nt(wr_hi, h_hi) + _dot_nt(wr_hi, h_lo) + _dot_nt(wr_lo, h_hi)
    return x1, h2, logits_t


def _prompt_mixer_kernel(*refs, n_seq, ret_heads, gla_heads):
    @pl.when(pl.program_id(0) < n_seq)
    def _():
        _prompt_mixer_body(*refs, ret_heads=ret_heads, gla_heads=gla_heads)

    @pl.when(pl.program_id(0) >= n_seq)
    def _():
        for out_ref in refs[16:20]:
            out_ref[...] = jnp.zeros_like(out_ref)


def _store_token_tiles(h3_ref, h2):
    rows, d = h2.shape
    nchunk = d // LANES
    for c in range(nchunk):
        h3_ref[pl.ds(c, rows, stride=nchunk), :] = h2[:, c * LANES:(c + 1) * LANES]


def _prompt_mixer_body(x_ref, mod_ref, npre_ref, npost_ref, npre2_ref, gbias_ref, rnw_ref, rnb_ref,
                       gnw_ref, win_ref, gup_ref, wout_ref, wrhi_ref, wrlo_ref, cos_ref, sin_ref,
                       x1_ref, h2_ref, lg_ref, h3_ref, sret_ref, sgla_ref,
                       proj_scr, mix_scr, dmat_scr, sret_scr, sgla_scr, *, ret_heads, gla_heads):
    t = pl.program_id(1)
    tt = x_ref.shape[1]
    rdk = 128
    rdv = 128
    gdk = gup_ref.shape[1] // gla_heads
    gdv = 128
    rq0, rk0, rv0, rg0 = 0, ret_heads * rdk, 2 * ret_heads * rdk, 2 * ret_heads * rdk + ret_heads * rdv
    gq0 = rg0 + ret_heads * rdv
    gk0 = gq0 + gla_heads * gdk
    gv0 = gk0 + gla_heads * gdk
    gg0 = gv0 + gla_heads * gdv
    ga0 = gg0 + gla_heads * gdv
    log_gamma = [math.log(1.0 - 2.0 ** (-5.0 - h)) for h in range(ret_heads)]

    @pl.when(jnp.logical_and(pl.program_id(0) == 0, t == 0))
    def _():
        i = lax.broadcasted_iota(jnp.int32, (tt, tt), 0)
        j = lax.broadcasted_iota(jnp.int32, (tt, tt), 1)
        for h in range(ret_heads):
            dmat_scr[h] = jnp.where(i >= j, jnp.exp(jnp.where(i >= j, (i - j).astype(F32) * log_gamma[h], 0.0)), 0.0)

    @pl.when(t == 0)
    def _():
        sret_scr[...] = jnp.zeros_like(sret_scr)
        sgla_scr[...] = jnp.zeros_like(sgla_scr)

    x = x_ref[0]
    mod = mod_ref[0]
    sh1, sc1, g1, sh2, sc2 = mod[0:1], mod[1:2], mod[2:3], mod[3:4], mod[4:5]
    h = _rms(x, npre_ref[...]) * (1.0 + sc1) + sh1
    proj_scr[...] = _dot(h.astype(BF16), win_ref[...])

    cos = cos_ref[...]
    sin = sin_ref[...]
    row = lax.broadcasted_iota(jnp.int32, (tt, rdk), 0).astype(F32)
    for hd in range(ret_heads):
        lg = log_gamma[hd]
        q = _rotary(proj_scr[:, rq0 + hd * rdk: rq0 + (hd + 1) * rdk], cos, sin)
        k = _rotary(proj_scr[:, rk0 + hd * rdk: rk0 + (hd + 1) * rdk], cos, sin) * (rdk ** -0.5)
        v = proj_scr[:, rv0 + hd * rdv: rv0 + (hd + 1) * rdv].astype(BF16)
        g = proj_scr[:, rg0 + hd * rdv: rg0 + (hd + 1) * rdv]
        qb = q.astype(BF16)
        s_old = sret_scr[hd]
        scores = (_dot_nt(qb, k.astype(BF16)) * dmat_scr[hd]).astype(BF16)
        o = _dot(scores, v) + jnp.exp((row + 1.0) * lg) * _dot(qb, s_old.astype(BF16))
        k_dec = (k * jnp.exp((tt - 1.0 - row) * lg)).astype(BF16)
        sret_scr[hd] = s_old * math.exp(tt * lg) + _dot_tn(k_dec, v)
        mu = jnp.mean(o, axis=-1, keepdims=True)
        oc = o - mu
        var = jnp.mean(oc * oc, axis=-1, keepdims=True)
        y = oc * lax.rsqrt(var + EPS) * rnw_ref[:, hd * rdv:(hd + 1) * rdv] + rnb_ref[:, hd * rdv:(hd + 1) * rdv]
        mix_scr[:, hd * rdv:(hd + 1) * rdv] = (_silu(g) * y).astype(BF16)

    c = GLA_CHUNK
    n_chunks = tt // c
    gw = gla_heads * gdk
    ga = proj_scr[:, ga0:ga0 + LANES].astype(BF16)
    logit = _dot(ga, gup_ref[...]) + gbias_ref[...]
    la = _log_sigmoid(logit) * (1.0 / GLA_GATE_NORM)
    ri = lax.broadcasted_iota(jnp.int32, (tt, tt), 0)
    ci = lax.broadcasted_iota(jnp.int32, (tt, tt), 1)
    csh = int(math.log2(c))
    tril = jnp.where((ri >= ci) & ((ri >> csh) == (ci >> csh)), 1.0, 0.0).astype(BF16)
    la_hi, la_mid, la_lo = _split3(la)
    bcum = _dot(tril, la_hi) + _dot(tril, la_mid) + _dot(tril, la_lo)
    levels = [s for s in (32, 16, 8, 4, 2, 1, 0) if s < c]
    masks = _gla_level_masks(c, gla_heads, levels)
    hm_k = _head_masks(c, gla_heads, gdk)
    hm_v = _head_masks(c, gla_heads, gdv)
    bd_mask = (lax.broadcasted_iota(jnp.int32, (gla_heads * gdv, gw), 0) >> int(math.log2(gdv))) == (
        lax.broadcasted_iota(jnp.int32, (gla_heads * gdv, gw), 1) >> int(math.log2(gdk)))
    for ch in range(n_chunks):
        r0 = ch * c
        q = proj_scr[r0:r0 + c, gq0:gq0 + gw] * (gdk ** -0.5)
        k = proj_scr[r0:r0 + c, gk0:gk0 + gw]
        v = proj_scr[r0:r0 + c, gv0:gv0 + gla_heads * gdv].astype(BF16)
        bc = bcum[r0:r0 + c]
        scores = _gla_intra_scores(q, k, bc, levels, masks, hm_k)
        o = _dot(scores.astype(BF16), _block_diag_rows(v, hm_v))
        st = sgla_scr[...]
        o = o + _dot_nt((q * jnp.exp(bc)).astype(BF16), st.astype(BF16))
        b_last = bc[c - 1:c, :]
        k_dec = (k * jnp.exp(b_last - bc)).astype(BF16)
        sgla_scr[...] = st * jnp.exp(b_last) + jnp.where(bd_mask, _dot_tn(v, k_dec), 0.0)
        gg = proj_scr[r0:r0 + c, gg0:gg0 + gla_heads * gdv]
        for hd in range(gla_heads):
            oh = o[:, hd * gdv:(hd + 1) * gdv]
            y = oh * lax.rsqrt(jnp.mean(oh * oh, axis=-1, keepdims=True) + EPS) * gnw_ref[:, hd * gdv:(hd + 1) * gdv]
            mix_scr[r0:r0 + c, ret_heads * rdv + hd * gdv: ret_heads * rdv + (hd + 1) * gdv] = (
                _silu(gg[:, hd * gdv:(hd + 1) * gdv]) * y).astype(BF16)

    x1, h2, logits_t = _mixer_tail(x, mix_scr[...], wout_ref[...], g1, npost_ref[...], npre2_ref[...],
                                   sc2, sh2, wrhi_ref[...], wrlo_ref[...])
    x1_ref[...] = x1
    h2_ref[...] = h2
    lg_ref[...] = logits_t
    _store_token_tiles(h3_ref, h2)

    @pl.when(t == pl.num_programs(1) - 1)
    def _():
        sret_ref[0] = sret_scr[...]
        sgla_ref[0] = sgla_scr[...]


def _prompt_mixer(x, mod, p, cos, sin, *, tile, n_total):
    b, t, d = x.shape
    nt = t // tile
    tiles = n_total // tile
    extra = -(-(tiles - b * nt) // nt)
    rh, gh = p["ret_heads"], p["gla_heads"]
    in_w = p["w_in"].shape[1]
    ne = p["wr_hi"].shape[0]
    gw = p["gup"].shape[1]
    full = lambda a: pl.BlockSpec(a.shape, lambda i, j: (0,) * a.ndim)
    vecs = [p["npre"], p["npost"], p["npre2"], p["gbias"], p["rnw"], p["rnb"], p["gnw"]]
    mats = [p["w_in"], p["gup"], p["w_out"], p["wr_hi"], p["wr_lo"]]
    kern = functools.partial(_prompt_mixer_kernel, n_seq=b, ret_heads=rh, gla_heads=gh)
    seq = lambda i: jnp.minimum(i, b - 1)
    out_tile = lambda i, j: jnp.minimum(i * nt + j, tiles - 1)
    return pl.pallas_call(
        kern,
        grid=(b + extra, nt),
        in_specs=[pl.BlockSpec((1, tile, d), lambda i, j: (seq(i), j, 0)),
                  pl.BlockSpec((1, 6, d), lambda i, j: (seq(i), 0, 0))]
                 + [full(a) for a in vecs] + [full(a) for a in mats]
                 + [pl.BlockSpec((tile, LANES), lambda i, j: (j, 0)),
                    pl.BlockSpec((tile, LANES), lambda i, j: (j, 0))],
        out_specs=[pl.BlockSpec((tile, d), lambda i, j: (out_tile(i, j), 0)),
                   pl.BlockSpec((tile, d), lambda i, j: (out_tile(i, j), 0)),
                   pl.BlockSpec((ne, tile), lambda i, j: (0, out_tile(i, j))),
                   pl.BlockSpec((tile * (d // LANES), LANES), lambda i, j: (out_tile(i, j), 0)),
                   pl.BlockSpec((1, rh, 128, 128), lambda i, j: (seq(i), 0, 0, 0)),
                   pl.BlockSpec((1, gh * 128, gw), lambda i, j: (seq(i), 0, 0))],
        out_shape=[jax.ShapeDtypeStruct((n_total, d), F32),
                   jax.ShapeDtypeStruct((n_total, d), F32),
                   jax.ShapeDtypeStruct((ne, n_total), F32),
                   jax.ShapeDtypeStruct((n_total * (d // LANES), LANES), F32),
                   jax.ShapeDtypeStruct((b, rh, 128, 128), F32),
                   jax.ShapeDtypeStruct((b, gh * 128, gw), F32)],
        scratch_shapes=[pltpu.VMEM((tile, in_w), F32),
                        pltpu.VMEM((tile, p["w_out"].shape[0]), BF16),
                        pltpu.VMEM((rh, tile, tile), F32),
                        pltpu.VMEM((rh, 128, 128), F32),
                        pltpu.VMEM((gh * 128, gw), F32)],
        compiler_params=pltpu.CompilerParams(
            dimension_semantics=("arbitrary", "arbitrary"), vmem_limit_bytes=VMEM_LIMIT_BYTES),
        name="prompt_mixer",
    )(x, mod, *vecs, *mats, cos, sin)


def _sample_mixer_kernel(x_ref, mod_ref, npre_ref, npost_ref, npre2_ref, gbias_ref, rnw_ref, rnb_ref,
                         gnw_ref, win_ref, gup_ref, wout_ref, wrhi_ref, wrlo_ref, cos_ref, sin_ref,
                         sret_in_ref, sgla_in_ref, x1_any, h2_any, lg_any, h3_any,
                         x1_ref, h2_ref, lg_ref, h3_ref, sret_ref, sgla_ref,
                         proj_scr, mix_scr, qrot_scr, krot_scr, oret_scr, bc_scr, gqe_scr, ogla_scr,
                         *, ret_heads, gla_heads, ts):
    del x1_any, h2_any, lg_any, h3_any
    rows = x_ref.shape[0]
    nb = rows // ts
    rdk = 128
    rdv = 128
    gdk = gup_ref.shape[1] // gla_heads
    gdv = 128
    gw = gla_heads * gdk
    gvw = gla_heads * gdv
    rq0, rk0, rv0, rg0 = 0, ret_heads * rdk, 2 * ret_heads * rdk, 2 * ret_heads * rdk + ret_heads * rdv
    gq0 = rg0 + ret_heads * rdv
    gk0 = gq0 + gw
    gv0 = gk0 + gw
    gg0 = gv0 + gvw
    ga0 = gg0 + gvw
    log_gamma = [math.log(1.0 - 2.0 ** (-5.0 - h)) for h in range(ret_heads)]
    tsh = int(math.log2(ts))

    def per_row(m):
        return jnp.broadcast_to(m[:, None, :], (nb, ts, m.shape[-1])).reshape(rows, m.shape[-1])

    x = x_ref[...]
    sh1, sc1, g1, sh2, sc2 = (per_row(mod_ref[i]) for i in range(5))
    h = _rms(x, npre_ref[...]) * (1.0 + sc1) + sh1
    proj_scr[...] = _dot(h.astype(BF16), win_ref[...])

    cos = cos_ref[...]
    sin = sin_ref[...]
    ri = lax.broadcasted_iota(jnp.int32, (rows, rows), 0)
    ci = lax.broadcasted_iota(jnp.int32, (rows, rows), 1)
    same = (ri >= ci) & ((ri >> tsh) == (ci >> tsh))
    for hd in range(ret_heads):
        lg = log_gamma[hd]
        q = _rotary(proj_scr[:, rq0 + hd * rdk: rq0 + (hd + 1) * rdk], cos, sin)
        k = _rotary(proj_scr[:, rk0 + hd * rdk: rk0 + (hd + 1) * rdk], cos, sin) * (rdk ** -0.5)
        v = proj_scr[:, rv0 + hd * rdv: rv0 + (hd + 1) * rdv].astype(BF16)
        qrot_scr[:, hd * rdk:(hd + 1) * rdk] = q
        krot_scr[:, hd * rdk:(hd + 1) * rdk] = k
        dmat = jnp.where(same, jnp.exp(jnp.where(same, (ri - ci).astype(F32) * lg, 0.0)), 0.0)
        scores = (_dot_nt(q.astype(BF16), k.astype(BF16)) * dmat).astype(BF16)
        oret_scr[:, hd * rdv:(hd + 1) * rdv] = _dot(scores, v)

    ga = proj_scr[:, ga0:ga0 + LANES].astype(BF16)
    logit = _dot(ga, gup_ref[...]) + gbias_ref[...]
    la = _log_sigmoid(logit) * (1.0 / GLA_GATE_NORM)
    tril = jnp.where(same, 1.0, 0.0).astype(BF16)
    la_hi, la_mid, la_lo = _split3(la)
    bcum = _dot(tril, la_hi) + _dot(tril, la_mid) + _dot(tril, la_lo)
    bc_scr[...] = bcum
    c = min(GLA_CHUNK, rows)
    levels = [s for s in (32, 16, 8, 4, 2, 1, 0) if s < ts]
    masks = _gla_level_masks(c, gla_heads, levels)
    hm_k = _head_masks(c, gla_heads, gdk)
    hm_v = _head_masks(c, gla_heads, gdv)
    for ch in range(rows // c):
        r0 = ch * c
        q = proj_scr[r0:r0 + c, gq0:gq0 + gw] * (gdk ** -0.5)
        k = proj_scr[r0:r0 + c, gk0:gk0 + gw]
        v = proj_scr[r0:r0 + c, gv0:gv0 + gvw].astype(BF16)
        bc = bcum[r0:r0 + c]
        scores = _gla_intra_scores(q, k, bc, levels, masks, hm_k)
        ogla_scr[r0:r0 + c, :] = _dot(scores.astype(BF16), _block_diag_rows(v, hm_v))
        gqe_scr[r0:r0 + c, :] = q * jnp.exp(bc)

    trow = lax.broadcasted_iota(jnp.int32, (ts, rdk), 0).astype(F32)
    eye = lax.broadcasted_iota(jnp.int32, (gw, gw), 0) == lax.broadcasted_iota(jnp.int32, (gw, gw), 1)
    zero_blk = jnp.zeros((gdk, gdv), BF16)

    def element(b, carry):
        rs = pl.ds(pl.multiple_of(b * ts, ts), ts)
        for hd in range(ret_heads):
            lg = log_gamma[hd]
            q = qrot_scr[rs, hd * rdk:(hd + 1) * rdk]
            k = krot_scr[rs, hd * rdk:(hd + 1) * rdk]
            v = proj_scr[rs, rv0 + hd * rdv: rv0 + (hd + 1) * rdv]
            s_old = sret_in_ref[b, hd]
            oret_scr[rs, hd * rdv:(hd + 1) * rdv] += jnp.exp((trow + 1.0) * lg) * _dot(
                q.astype(BF16), s_old.astype(BF16))
            k_dec = (k * jnp.exp((ts - 1.0 - trow) * lg)).astype(BF16)
            sret_ref[b, hd] = s_old * math.exp(ts * lg) + _dot_tn(k_dec, v.astype(BF16))
        s_b = sgla_in_ref[b]
        s_bd = jnp.concatenate(
            [jnp.concatenate([s_b[hd].astype(BF16) if h2 == hd else zero_blk for h2 in range(gla_heads)], axis=1)
             for hd in range(gla_heads)], axis=0)
        ogla_scr[rs, :] += _dot(gqe_scr[rs, :].astype(BF16), s_bd)
        bc = bc_scr[rs, :]
        b_last = bc[ts - 1:ts, :]
        k_dec = (proj_scr[rs, gk0:gk0 + gw] * jnp.exp(b_last - bc)).astype(BF16)
        upd = _dot_tn(k_dec, proj_scr[rs, gv0:gv0 + gvw].astype(BF16))
        decay_col = jnp.sum(jnp.where(eye, jnp.broadcast_to(jnp.exp(b_last), (gw, gw)), 0.0), axis=1, keepdims=True)
        for hd in range(gla_heads):
            sgla_ref[b, hd] = (s_b[hd] * decay_col[hd * gdk:(hd + 1) * gdk]
                               + upd[hd * gdk:(hd + 1) * gdk, hd * gdv:(hd + 1) * gdv])
        return carry

    lax.fori_loop(0, nb, element, 0)

    for hd in range(ret_heads):
        o = oret_scr[:, hd * rdv:(hd + 1) * rdv]
        g = proj_scr[:, rg0 + hd * rdv: rg0 + (hd + 1) * rdv]
        mu = jnp.mean(o, axis=-1, keepdims=True)
        oc = o - mu
        var = jnp.mean(oc * oc, axis=-1, keepdims=True)
        y = oc * lax.rsqrt(var + EPS) * rnw_ref[:, hd * rdv:(hd + 1) * rdv] + rnb_ref[:, hd * rdv:(hd + 1) * rdv]
        mix_scr[:, hd * rdv:(hd + 1) * rdv] = (_silu(g) * y).astype(BF16)
    for hd in range(gla_heads):
        oh = ogla_scr[:, hd * gdv:(hd + 1) * gdv]
        gg = proj_scr[:, gg0 + hd * gdv: gg0 + (hd + 1) * gdv]
        y = oh * lax.rsqrt(jnp.mean(oh * oh, axis=-1, keepdims=True) + EPS) * gnw_ref[:, hd * gdv:(hd + 1) * gdv]
        mix_scr[:, ret_heads * rdv + hd * gdv: ret_heads * rdv + (hd + 1) * gdv] = (_silu(gg) * y).astype(BF16)

    x1, h2, logits_t = _mixer_tail(x, mix_scr[...], wout_ref[...], g1, npost_ref[...], npre2_ref[...],
                                   sc2, sh2, wrhi_ref[...], wrlo_ref[...])
    x1_ref[...] = x1
    h2_ref[...] = h2
    lg_ref[...] = logits_t
    _store_token_tiles(h3_ref, h2)


def _sample_mixer(x, mod_t, p, cos, sin, state_ret, state_gla, x1_buf, h2_buf, lg_buf, h3_buf, *, ts, tile):
    n, d = x.shape
    off = (x1_buf.shape[0] - n) // tile
    nb = tile // ts
    rh, gh = p["ret_heads"], p["gla_heads"]
    in_w = p["w_in"].shape[1]
    ne = p["wr_hi"].shape[0]
    gw = p["gup"].shape[1]
    gdk = gw // gh
    full = lambda a: pl.BlockSpec(a.shape, lambda i: (0,) * a.ndim)
    vecs = [p["npre"], p["npost"], p["npre2"], p["gbias"], p["rnw"], p["rnb"], p["gnw"]]
    mats = [p["w_in"], p["gup"], p["w_out"], p["wr_hi"], p["wr_lo"]]
    kern = functools.partial(_sample_mixer_kernel, ret_heads=rh, gla_heads=gh, ts=ts)
    n_in = 2 + len(vecs) + len(mats) + 4 + 4
    return pl.pallas_call(
        kern,
        grid=(n // tile,),
        in_specs=[pl.BlockSpec((tile, d), lambda i: (i, 0)),
                  pl.BlockSpec((6, nb, d), lambda i: (0, i, 0))]
                 + [full(a) for a in vecs] + [full(a) for a in mats]
                 + [full(cos), full(sin),
                    pl.BlockSpec((nb, rh, 128, 128), lambda i: (i, 0, 0, 0)),
                    pl.BlockSpec((nb, gh, gdk, 128), lambda i: (i, 0, 0, 0)),
                    pl.BlockSpec(memory_space=pl.ANY), pl.BlockSpec(memory_space=pl.ANY),
                    pl.BlockSpec(memory_space=pl.ANY), pl.BlockSpec(memory_space=pl.ANY)],
        out_specs=[pl.BlockSpec((tile, d), lambda i: (off + i, 0)),
                   pl.BlockSpec((tile, d), lambda i: (off + i, 0)),
                   pl.BlockSpec((ne, tile), lambda i: (0, off + i)),
                   pl.BlockSpec((tile * (d // LANES), LANES), lambda i: (off + i, 0)),
                   pl.BlockSpec((nb, rh, 128, 128), lambda i: (i, 0, 0, 0)),
                   pl.BlockSpec((nb, gh, gdk, 128), lambda i: (i, 0, 0, 0))],
        out_shape=[jax.ShapeDtypeStruct(x1_buf.shape, F32),
                   jax.ShapeDtypeStruct(h2_buf.shape, F32),
                   jax.ShapeDtypeStruct(lg_buf.shape, F32),
                   jax.ShapeDtypeStruct(h3_buf.shape, F32),
                   jax.ShapeDtypeStruct(state_ret.shape, F32),
                   jax.ShapeDtypeStruct(state_gla.shape, F32)],
        input_output_aliases={n_in - 4: 0, n_in - 3: 1, n_in - 2: 2, n_in - 1: 3},
        scratch_shapes=[pltpu.VMEM((tile, in_w), F32),
                        pltpu.VMEM((tile, p["w_out"].shape[0]), BF16),
                        pltpu.VMEM((tile, rh * 128), F32),
                        pltpu.VMEM((tile, rh * 128), F32),
                        pltpu.VMEM((tile, rh * 128), F32),
                        pltpu.VMEM((tile, gw), F32),
                        pltpu.VMEM((tile, gw), F32),
                        pltpu.VMEM((tile, gh * 128), F32)],
        compiler_params=pltpu.CompilerParams(
            dimension_semantics=("arbitrary",), vmem_limit_bytes=VMEM_LIMIT_BYTES),
        name="sample_mixer",
    )(x, mod_t, *vecs, *mats, cos, sin, state_ret, state_gla, x1_buf, h2_buf, lg_buf, h3_buf)


def _router_kernel(lg_ref, bias_ref, idx_ref, w_ref, cnt_ref, cnt_scr, *, n_groups, topk_groups, top_k):
    ne, tn = lg_ref.shape
    gsz = ne // n_groups
    neg = -jnp.inf

    @pl.when(pl.program_id(1) == 0)
    def _():
        cnt_scr[...] = jnp.zeros_like(cnt_scr)

    scores = jax.nn.sigmoid(lg_ref[...])
    sel = scores + bias_ref[...][:, 0:1]
    sel3 = sel.reshape(n_groups, gsz, tn)
    mem = lax.broadcasted_iota(jnp.int32, (n_groups, gsz, tn), 1)
    m1 = jnp.max(sel3, axis=1, keepdims=True)
    first = jnp.min(jnp.where(sel3 == m1, mem, gsz), axis=1, keepdims=True)
    m2 = jnp.max(jnp.where(mem == first, neg, sel3), axis=1, keepdims=True)
    gscore = (m1 + m2).reshape(n_groups, tn)
    gi = lax.broadcasted_iota(jnp.int32, (n_groups, tn), 0)
    gsel = jnp.zeros((n_groups, tn), jnp.bool_)
    work = gscore
    for _ in range(topk_groups):
        mx = jnp.max(work, axis=0, keepdims=True)
        pick = gi == jnp.min(jnp.where(work == mx, gi, n_groups), axis=0, keepdims=True)
        gsel = jnp.logical_or(gsel, pick)
        work = jnp.where(pick, neg, work)
    emask = jnp.broadcast_to(gsel[:, None, :], (n_groups, gsz, tn)).reshape(ne, tn)
    ei = lax.broadcasted_iota(jnp.int32, (ne, tn), 0)
    work = jnp.where(emask, sel, neg)
    chosen_any = jnp.zeros((ne, tn), jnp.bool_)
    idx_rows, w_rows = [], []
    for _ in range(top_k):
        mx = jnp.max(work, axis=0, keepdims=True)
        first_e = jnp.min(jnp.where(work == mx, ei, ne), axis=0, keepdims=True)
        pick = ei == first_e
        chosen_any = jnp.logical_or(chosen_any, pick)
        work = jnp.where(pick, neg, work)
        idx_rows.append(first_e)
        w_rows.append(jnp.sum(jnp.where(pick, scores, 0.0), axis=0, keepdims=True))
    wsum = w_rows[0]
    for r in w_rows[1:]:
        wsum = wsum + r
    idx_ref[...] = jnp.concatenate(idx_rows, axis=0)
    w_ref[...] = jnp.concatenate(w_rows, axis=0) / wsum * ROUTED_SCALE
    cnt_scr[...] += _dot(jnp.where(chosen_any, 1.0, 0.0).astype(BF16), jnp.ones((tn, LANES), BF16))
    cnt_ref[0] = cnt_scr[...]


def _router(logits_t, bias, *, parts, tile):
    ne, n = logits_t.shape
    tiles = n // parts // tile
    kern = functools.partial(_router_kernel, n_groups=N_GROUPS, topk_groups=TOPK_GROUPS, top_k=TOP_K)
    return pl.pallas_call(
        kern,
        grid=(parts, tiles),
        in_specs=[pl.BlockSpec((ne, tile), lambda p, j: (0, p * tiles + j)),
                  pl.BlockSpec((ne, LANES), lambda p, j: (0, 0))],
        out_specs=[pl.BlockSpec((TOP_K, tile), lambda p, j: (0, p * tiles + j)),
                   pl.BlockSpec((TOP_K, tile), lambda p, j: (0, p * tiles + j)),
                   pl.BlockSpec((1, ne, LANES), lambda p, j: (p, 0, 0))],
        out_shape=[jax.ShapeDtypeStruct((TOP_K, n), jnp.int32),
                   jax.ShapeDtypeStruct((TOP_K, n), F32),
                   jax.ShapeDtypeStruct((parts, ne, LANES), F32)],
        scratch_shapes=[pltpu.VMEM((ne, LANES), F32)],
        compiler_params=pltpu.CompilerParams(
            dimension_semantics=("arbitrary", "arbitrary"), vmem_limit_bytes=VMEM_LIMIT_BYTES),
        name="router_topk",
    )(logits_t, jnp.broadcast_to(bias.reshape(ne, 1), (ne, LANES)))


def _dispatch_plan(idx_t, w_t, counts, *, parts, blk, rows_per_token):
    k, n = idx_t.shape
    npart = n // parts
    ne = counts.shape[1]
    stride = npart + blk
    big = ne * stride
    local = jnp.arange(n, dtype=jnp.int32) % npart
    to_parts = lambda a: a.reshape(k, parts, npart).transpose(1, 0, 2).reshape(parts, k * npart)
    keys = to_parts(idx_t * stride + local[None, :])
    wts = to_parts(w_t)
    need = (-counts) % blk
    j = jnp.arange(blk - 1, dtype=jnp.int32)
    pad_keys = jnp.where(j[None, None, :] < need[:, :, None],
                         jnp.arange(ne, dtype=jnp.int32)[None, :, None] * stride + npart + j[None, None, :], big)
    total = k * npart + ne * (blk - 1)
    nblk = -(-total // blk)
    fill = nblk * blk - total
    all_keys = jnp.concatenate([keys, pad_keys.reshape(parts, -1), jnp.full((parts, fill), big, jnp.int32)], axis=1)
    all_w = jnp.concatenate([wts, jnp.zeros((parts, nblk * blk - k * npart), F32)], axis=1)
    sk, sw = lax.sort((all_keys, all_w), dimension=1, num_keys=1)
    valid = sk < big
    slot_tok = sk % stride
    real = jnp.logical_and(valid, slot_tok < npart)
    acc_row = jnp.where(real, slot_tok, npart) * rows_per_token
    src_row = jnp.where(real, slot_tok, 0) * rows_per_token
    w = jnp.where(real, sw, 0.0)
    blocks_per_expert = (counts + blk - 1) // blk
    first_blk = jnp.concatenate([jnp.zeros((parts, 1), jnp.int32),
                                 jnp.cumsum(blocks_per_expert, axis=1, dtype=jnp.int32)], axis=1)
    pad_src = jnp.zeros((parts, 2 * blk), jnp.int32)
    pad_acc = jnp.full((parts, 2 * blk), npart * rows_per_token, jnp.int32)
    pad_w = jnp.zeros((parts, 2 * blk), F32)
    src_t = jnp.concatenate([src_row, pad_src], axis=1)
    acc_t = jnp.concatenate([pad_acc, acc_row], axis=1)
    w_t2 = jnp.concatenate([pad_w, w], axis=1)
    return src_t.reshape(-1), acc_t.reshape(-1), w_t2.reshape(-1), first_blk, nblk


def _moe_kernel(first_ref, src_hbm, dst_hbm, wts_hbm, h3_ref, h2_ref, x1_ref, g2_ref, npost_ref,
                wg_ref, wu_ref, wd_ref, wsg_ref, wsu_ref, wsd_ref, ya_ref, yb_ref,
                acc_scr, g_scr, y2_scr, xs_scr, wgu_scr, wdn_scr, src_smem, dst_smem, wts_smem, sem,
                *, ne, nblk, blk, ts, n_epi, tiles_a):
    p = pl.program_id(0)
    b = pl.program_id(1)
    d = h2_ref.shape[1]
    ff = wg_ref.shape[2]
    nchunk = d // LANES
    tile_at = lambda row: pl.ds(pl.multiple_of(row, nchunk), nchunk)

    depth = src_smem.shape[0]
    n_valid = first_ref[p, ne]
    last_t = n_valid + 1

    def table_copies(t):
        slot = t & (depth - 1)
        row = pl.multiple_of((p * (nblk + 2) + t) * blk, blk)
        return tuple(pltpu.make_async_copy(hbm.at[pl.ds(row, blk)], smem.at[slot], sem.at[i, slot])
                     for i, (hbm, smem) in enumerate(((src_hbm, src_smem), (dst_hbm, dst_smem), (wts_hbm, wts_smem))))

    def gather(slot, xs_ref):
        for r in range(blk):
            g_scr[r * nchunk:(r + 1) * nchunk, :] = h3_ref[tile_at(src_smem[slot, r]), :]
        for c in range(nchunk):
            xs_ref[:, c * LANES:(c + 1) * LANES] = g_scr[pl.ds(c, blk, stride=nchunk), :].astype(BF16)

    def expert_ffn(xs_ref, y2_ref):
        gu = _dot(xs_ref[...], wgu_scr[...])
        hid = (_silu(gu[:, :ff]) * gu[:, ff:]).astype(BF16)
        y = _dot(hid, wdn_scr[...])
        for c in range(nchunk):
            y2_ref[pl.ds(c, blk, stride=nchunk), :] = y[:, c * LANES:(c + 1) * LANES]

    def scatter_add(slot, y2_ref):
        for g in range(blk // SUBLANES):
            rows = [g * SUBLANES + i for i in range(SUBLANES)]
            dsts = [dst_smem[slot, r] for r in rows]
            new = [acc_scr[tile_at(t), :] + wts_smem[slot, r] * y2_ref[r * nchunk:(r + 1) * nchunk, :]
                   for t, r in zip(dsts, rows)]
            for t, v in zip(dsts, new):
                acc_scr[tile_at(t), :] = v

    @pl.when(b == 0)
    def _():
        acc_scr[...] = jnp.zeros_like(acc_scr)
        y2_scr[...] = jnp.zeros_like(y2_scr)
        for t in range(depth):
            @pl.when(t <= last_t)
            def _():
                for cp in table_copies(t):
                    cp.start()
        for cp in table_copies(0):
            cp.wait()
        gather(0, xs_scr.at[0])

    def pipeline_step(f, carry):
        t = f + 1
        slot = t & (depth - 1)
        for cp in table_copies(t):
            cp.wait()

        @pl.when(t + (depth - 1) <= last_t)
        def _():
            for cp in table_copies(t + (depth - 1)):
                cp.start()

        for parity in range(2):
            @pl.when((t & 1) == parity)
            def _():
                gather(slot, xs_scr.at[parity])
                expert_ffn(xs_scr.at[1 - parity], y2_scr.at[1 - parity])
                scatter_add(slot, y2_scr.at[parity])
        return carry

    @pl.when(b < ne)
    def _():
        e = jnp.minimum(b, ne - 1)
        f0, f1 = first_ref[p, e], first_ref[p, e + 1]

        @pl.when(f1 > f0)
        def _():
            wgu_scr[:, :ff] = wg_ref[0].astype(BF16)
            wgu_scr[:, ff:] = wu_ref[0].astype(BF16)
            wdn_scr[...] = wd_ref[0].astype(BF16)
            lax.fori_loop(f0, f1, pipeline_step, 0)

    @pl.when(b == ne)
    def _():
        for cp in table_copies(last_t):
            cp.wait()
        for parity in range(2):
            @pl.when((last_t & 1) == parity)
            def _():
                scatter_add(last_t & (depth - 1), y2_scr.at[parity])

    @pl.when(b >= ne)
    def _():
        rows = ya_ref.shape[0]
        t0 = pl.multiple_of((b - ne) * rows, rows)
        routed = jnp.concatenate(
            [acc_scr[pl.ds(t0 * nchunk + c, rows, stride=nchunk), :] for c in range(nchunk)], axis=1)
        hs = h2_ref[...].astype(BF16)
        hid = (_silu(_dot(hs, wsg_ref[...])) * _dot(hs, wsu_ref[...])).astype(BF16)
        f = routed + _dot(hid, wsd_ref[...])
        g2 = g2_ref[0]
        g2 = jnp.broadcast_to(g2[:, None, :], (rows // ts, ts, d)).reshape(rows, d)
        y = x1_ref[...] + g2 * _rms(f, npost_ref[...])
        tile = p * n_epi + (b - ne)

        @pl.when(tile < tiles_a)
        def _():
            ya_ref[...] = y

        @pl.when(tile >= tiles_a)
        def _():
            yb_ref[...] = y


def _moe(h2, h3, x1, g2_tiles, npost, plan, w_gate, w_up, w_down, wsg, wsu, wsd, *, parts, blk, ts, n_a):
    src_row, acc_row, wts, first_blk, nblk = plan
    n, d = h2.shape
    npart = n // parts
    ne, _, ff = w_gate.shape
    epi = LANES
    n_epi = npart // epi
    tiles_a = n_a // epi
    tiles_b = (n - n_a) // epi
    nchunk = d // LANES
    kern = functools.partial(_moe_kernel, ne=ne, nblk=nblk, blk=blk, ts=ts, n_epi=n_epi, tiles_a=tiles_a)
    epi_of = lambda p, b: p * n_epi + jnp.clip(b - ne, 0, n_epi - 1)
    exp_of = lambda p, b: jnp.minimum(b, ne - 1)
    full = lambda a: pl.BlockSpec(a.shape, lambda p, b, fb: (0,) * a.ndim)
    grid_spec = pltpu.PrefetchScalarGridSpec(
        num_scalar_prefetch=1,
        grid=(parts, ne + n_epi),
        in_specs=[
            pl.BlockSpec(memory_space=pl.ANY),
            pl.BlockSpec(memory_space=pl.ANY),
            pl.BlockSpec(memory_space=pl.ANY),
            pl.BlockSpec((npart * nchunk, LANES), lambda p, b, fb: (p, 0), pipeline_mode=pl.Buffered(1)),
            pl.BlockSpec((epi, d), lambda p, b, fb: (epi_of(p, b), 0)),
            pl.BlockSpec((epi, d), lambda p, b, fb: (epi_of(p, b), 0)),
            pl.BlockSpec((1, epi // ts, d), lambda p, b, fb: (epi_of(p, b), 0, 0)),
            full(npost),
            pl.BlockSpec((1, d, ff), lambda p, b, fb: (exp_of(p, b), 0, 0)),
            pl.BlockSpec((1, d, ff), lambda p, b, fb: (exp_of(p, b), 0, 0)),
            pl.BlockSpec((1, ff, d), lambda p, b, fb: (exp_of(p, b), 0, 0)),
            full(wsg), full(wsu), full(wsd),
        ],
        out_specs=[
            pl.BlockSpec((epi, d), lambda p, b, fb: (jnp.minimum(epi_of(p, b), tiles_a - 1), 0)),
            pl.BlockSpec((epi, d), lambda p, b, fb: (jnp.clip(epi_of(p, b) - tiles_a, 0, tiles_b - 1), 0))],
        scratch_shapes=[pltpu.VMEM(((npart + 1) * (d // LANES), LANES), F32),
                        pltpu.VMEM((blk * (d // LANES), LANES), F32),
                        pltpu.VMEM((2, blk * (d // LANES), LANES), F32),
                        pltpu.VMEM((2, blk, d), BF16),
                        pltpu.VMEM((d, 2 * ff), BF16),
                        pltpu.VMEM((ff, d), BF16),
                        pltpu.SMEM((MOE_TABLE_DEPTH, blk), jnp.int32),
                        pltpu.SMEM((MOE_TABLE_DEPTH, blk), jnp.int32),
                        pltpu.SMEM((MOE_TABLE_DEPTH, blk), F32),
                        pltpu.SemaphoreType.DMA((3, MOE_TABLE_DEPTH))],
    )
    return pl.pallas_call(
        kern,
        grid_spec=grid_spec,
        out_shape=[jax.ShapeDtypeStruct((n_a, d), F32), jax.ShapeDtypeStruct((n - n_a, d), F32)],
        compiler_params=pltpu.CompilerParams(
            dimension_semantics=("arbitrary", "arbitrary"), vmem_limit_bytes=VMEM_LIMIT_BYTES),
        name="moe_experts",
    )(first_blk, src_row, acc_row, wts, h3, h2, x1, g2_tiles, npost, w_gate, w_up, w_down, wsg, wsu, wsd)


def _rope_tables(pos):
    half = 64
    inv_freq = ROPE_BASE ** (-jnp.arange(half, dtype=F32) / half)
    ang = pos[:, None] * inv_freq[None, :]
    cos, sin = jnp.cos(ang), jnp.sin(ang)
    return jnp.concatenate([cos, cos], axis=1), jnp.concatenate([-sin, sin], axis=1)


def _mixer_params(norm_pre_mix, norm_post_mix, norm_pre_ffn, w_in, gla_gate_up, gla_gate_bias,
                  ret_norm_w, ret_norm_b, gla_norm_w, w_out, w_router):
    d, in_w = w_in.shape
    pad = (-in_w) % LANES
    rank = gla_gate_up.shape[0]
    wr_hi, wr_lo = _split2(w_router.T)
    return dict(
        ret_heads=ret_norm_w.shape[0], gla_heads=gla_norm_w.shape[0],
        npre=norm_pre_mix.reshape(1, d), npost=norm_post_mix.reshape(1, d), npre2=norm_pre_ffn.reshape(1, d),
        gbias=gla_gate_bias.reshape(1, -1), rnw=ret_norm_w.reshape(1, -1), rnb=ret_norm_b.reshape(1, -1),
        gnw=gla_norm_w.reshape(1, -1),
        w_in=jnp.pad(w_in, ((0, 0), (0, pad))).astype(BF16),
        gup=jnp.pad(gla_gate_up, ((0, LANES - rank), (0, 0))).astype(BF16),
        w_out=w_out.astype(BF16), wr_hi=wr_hi, wr_lo=wr_lo)


def kernel(x_prompt, x_sample, state_ret, state_gla, c_prompt, c_sample, w_ada, b_ada, norm_pre_mix, norm_post_mix, norm_pre_ffn, norm_post_ffn, w_in, gla_gate_up, gla_gate_bias, ret_norm_w, ret_norm_b, gla_norm_w, w_out, w_router, router_bias, w_exp_gate, w_exp_up, w_exp_down, w_sh_gate, w_sh_up, w_sh_down):
    bp, tp, d = x_prompt.shape
    bs, ts, _ = x_sample.shape
    l = 0
    c_all = jnp.concatenate([c_prompt, c_sample], axis=0)
    mod = _ada(c_all, w_ada[l], b_ada[l]).reshape(bp + bs, 6, d)
    p = _mixer_params(norm_pre_mix[l], norm_post_mix[l], norm_pre_ffn[l], w_in[l], gla_gate_up[l],
                      gla_gate_bias[l], ret_norm_w[l], ret_norm_b[l], gla_norm_w[l], w_out[l], w_router[l])
    n_p, n_s = bp * tp, bs * ts
    n = n_p + n_s
    cos_p, sin_p = _rope_tables(jnp.arange(tp, dtype=F32))
    x1, h2, logits_t, h3, sret_p, sgla_p = _prompt_mixer(x_prompt, mod[:bp], p, cos_p, sin_p,
                                                         tile=PROMPT_TILE, n_total=n)
    cos_s, sin_s = _rope_tables(PAST_LEN + jnp.arange(ts, dtype=F32))
    cos_s, sin_s = jnp.tile(cos_s, (SAMPLE_TILE // ts, 1)), jnp.tile(sin_s, (SAMPLE_TILE // ts, 1))
    x1, h2, logits_t, h3, sret_s, sgla_s = _sample_mixer(
        x_sample.reshape(n_s, d), mod[bp:].transpose(1, 0, 2), p, cos_s, sin_s, state_ret[l], state_gla[l],
        x1, h2, logits_t, h3, ts=ts, tile=SAMPLE_TILE)
    gh, gdk = p["gla_heads"], p["gup"].shape[1] // p["gla_heads"]
    sgla_blocks = sgla_p.reshape(bp, gh, 128, gh, gdk)
    new_gla_p = jnp.stack([sgla_blocks[:, h, :, h, :] for h in range(gh)], axis=1).transpose(0, 1, 3, 2)

    idx_t, w_t, cnt = _router(logits_t, router_bias[l], parts=MOE_PARTS, tile=ROUTER_TILE)
    plan = _dispatch_plan(idx_t, w_t, cnt[:, :, 0].astype(jnp.int32), parts=MOE_PARTS, blk=MOE_BLOCK,
                          rows_per_token=d // LANES)
    per_tile = LANES // ts
    g2 = mod[:, 5, :]
    g2_tiles = jnp.concatenate(
        [jnp.broadcast_to(jnp.repeat(g2[:bp], tp // LANES, axis=0)[:, None, :], (n_p // LANES, per_tile, d)),
         g2[bp:].reshape(n_s // LANES, per_tile, d)], axis=0)
    y_p, y_s = _moe(h2, h3, x1, g2_tiles, norm_post_ffn[l].reshape(1, d), plan,
                    w_exp_gate[l], w_exp_up[l], w_exp_down[l],
                    w_sh_gate[l].astype(BF16), w_sh_up[l].astype(BF16), w_sh_down[l].astype(BF16),
                    parts=MOE_PARTS, blk=MOE_BLOCK, ts=ts, n_a=n_p)
    return (y_p.reshape(bp, tp, d), y_s.reshape(bs, ts, d), sret_p[None], new_gla_p[None],
            sret_s[None], sgla_s[None])
```

```python
import functools
import math

import jax
import jax.numpy as jnp
from jax import lax
from jax.experimental import pallas as pl
from jax.experimental.pallas import tpu as pltpu

F32 = jnp.float32
BF16 = jnp.bfloat16

PAST_LEN = 16384
ROPE_BASE = 10000.0
GLA_GATE_NORM = 16.0
TOP_K = 8
N_GROUPS = 8
TOPK_GROUPS = 4
ROUTED_SCALE = 2.5
EPS = 1e-6

LANES = 128
SUBLANES = 8
VMEM_LIMIT_BYTES = 56 * 1024 * 1024

GLA_CHUNK = 64
PROMPT_TILE = 256
SAMPLE_TILE = 128
ROUTER_TILE = 256
MOE_PARTS = 4
MOE_BLOCK = 128
MOE_TABLE_DEPTH = 8


def _dot(a, b):
    return jnp.dot(a, b, preferred_element_type=F32)


def _dot_nt(a, b):
    return lax.dot_general(a, b, (((1,), (1,)), ((), ())), preferred_element_type=F32)


def _dot_tn(a, b):
    return lax.dot_general(a, b, (((0,), (0,)), ((), ())), preferred_element_type=F32)


def _split2(x):
    hi = x.astype(BF16)
    lo = (x - hi.astype(F32)).astype(BF16)
    return hi, lo


def _split3(x):
    hi = x.astype(BF16)
    r = x - hi.astype(F32)
    mid = r.astype(BF16)
    lo = (r - mid.astype(F32)).astype(BF16)
    return hi, mid, lo


def _rms(x, g):
    return x * lax.rsqrt(jnp.mean(x * x, axis=-1, keepdims=True) + EPS) * g


def _silu(x):
    return x * jax.nn.sigmoid(x)


def _log_sigmoid(x):
    return jnp.minimum(x, 0.0) - jnp.log(1.0 + jnp.exp(-jnp.abs(x)))


def _ada_kernel(c_ref, w_ref, b_ref, o_ref):
    a_hi, a_lo = _split2(_silu(c_ref[...]))
    w_hi, w_lo = _split2(w_ref[...])
    o_ref[...] = _dot(a_hi, w_hi) + _dot(a_hi, w_lo) + _dot(a_lo, w_hi) + b_ref[...]


def _ada(c, w, b):
    rows, d = c.shape
    n = w.shape[1]
    tn = 1024
    return pl.pallas_call(
        _ada_kernel,
        grid=(n // tn,),
        in_specs=[
            pl.BlockSpec((rows, d), lambda j: (0, 0)),
            pl.BlockSpec((d, tn), lambda j: (0, j)),
            pl.BlockSpec((1, tn), lambda j: (0, j)),
        ],
        out_specs=pl.BlockSpec((rows, tn), lambda j: (0, j)),
        out_shape=jax.ShapeDtypeStruct((rows, n), F32),
        compiler_params=pltpu.CompilerParams(
            dimension_semantics=("arbitrary",), vmem_limit_bytes=VMEM_LIMIT_BYTES),
        name="ada_mod",
    )(c, w, b.reshape(1, n))


def _rotary(x, cos, sin_signed):
    return x * cos + pltpu.roll(x, x.shape[-1] // 2, axis=1) * sin_signed


def _level_reference(bc, s):
    rows, width = bc.shape
    pieces = []
    sub = lax.broadcasted_iota(jnp.int32, (SUBLANES, width), 0)
    for g in range(rows // SUBLANES):
        base = g * SUBLANES
        bounds = sorted({((base + r) // (2 * s)) * (2 * s) + s - 1 for r in range(SUBLANES)})
        piece = jnp.broadcast_to(bc[bounds[-1]:bounds[-1] + 1, :], (SUBLANES, width))
        for bm in reversed(bounds[:-1]):
            last_row_of_pair = bm + s - base
            piece = jnp.where(sub <= last_row_of_pair,
                              jnp.broadcast_to(bc[bm:bm + 1, :], (SUBLANES, width)), piece)
        pieces.append(piece)
    return jnp.concatenate(pieces, axis=0)


def _gla_level_masks(rows, heads, levels):
    i = lax.broadcasted_iota(jnp.int32, (rows, heads * rows), 0)
    j = lax.broadcasted_iota(jnp.int32, (rows, heads * rows), 1) & (rows - 1)
    masks = {}
    for s in levels:
        if s == 0:
            masks[s] = i == j
        else:
            sh = int(math.log2(s))
            masks[s] = ((i >> sh) == (j >> sh) + 1) & ((i >> (sh + 1)) == (j >> (sh + 1)))
    return masks


def _gla_intra_scores(q, k, bc, levels, masks, head_masks):
    scores = None
    for s in levels:
        if s == 0:
            qt, kt = q, k
        else:
            ref = _level_reference(bc, s)
            qt = q * jnp.exp(jnp.minimum(bc - ref, 0.0))
            kt = k * jnp.exp(jnp.minimum(ref - bc, 0.0))
        rhs_t = _block_diag_rows(kt.astype(BF16), head_masks)
        sc = jnp.where(masks[s], _dot_nt(qt.astype(BF16), rhs_t), 0.0)
        scores = sc if scores is None else scores + sc
    return scores


def _block_diag_rows(x, col_masks):
    return jnp.concatenate([x * m for m in col_masks], axis=0)


def _head_masks(rows, heads, width):
    lane = lax.broadcasted_iota(jnp.int32, (rows, heads * width), 1)
    sh = int(math.log2(width))
    return [jnp.where((lane >> sh) == h, 1.0, 0.0).astype(BF16) for h in range(heads)]


def _mixer_tail(x, mix_bf16, w_out, g1, npost, npre2, sc2, sh2, wr_hi, wr_lo):
    mix = _dot(mix_bf16, w_out)
    x1 = x + g1 * _rms(mix, npost)
    h2 = _rms(x1, npre2) * (1.0 + sc2) + sh2
    h_hi, h_lo = _split2(h2)
    logits_t = _dot_nt(wr_hi, h_hi) + _dot_nt(wr_hi, h_lo) + _dot_nt(wr_lo, h_hi)
    return x1, h2, logits_t


def _prompt_mixer_kernel(*refs, n_seq, ret_heads, gla_heads):
    @pl.when(pl.program_id(0) < n_seq)
    def _():
        _prompt_mixer_body(*refs, ret_heads=ret_heads, gla_heads=gla_heads)

    @pl.when(pl.program_id(0) >= n_seq)
    def _():
        for out_ref in refs[16:20]:
            out_ref[...] = jnp.zeros_like(out_ref)


def _store_token_tiles(h3_ref, h2):
    rows, d = h2.shape
    nchunk = d // LANES
    for c in range(nchunk):
        h3_ref[pl.ds(c, rows, stride=nchunk), :] = h2[:, c * LANES:(c + 1) * LANES]


def _prompt_mixer_body(x_ref, mod_ref, npre_ref, npost_ref, npre2_ref, gbias_ref, rnw_ref, rnb_ref,
                       gnw_ref, win_ref, gup_ref, wout_ref, wrhi_ref, wrlo_ref, cos_ref, sin_ref,
                       x1_ref, h2_ref, lg_ref, h3_ref, sret_ref, sgla_ref,
                       proj_scr, mix_scr, dmat_scr, sret_scr, sgla_scr, *, ret_heads, gla_heads):
    t = pl.program_id(1)
    tt = x_ref.shape[1]
    rdk = 128
    rdv = 128
    gdk = gup_ref.shape[1] // gla_heads
    gdv = 128
    rq0, rk0, rv0, rg0 = 0, ret_heads * rdk, 2 * ret_heads * rdk, 2 * ret_heads * rdk + ret_heads * rdv
    gq0 = rg0 + ret_heads * rdv
    gk0 = gq0 + gla_heads * gdk
    gv0 = gk0 + gla_heads * gdk
    gg0 = gv0 + gla_heads * gdv
    ga0 = gg0 + gla_heads * gdv
    log_gamma = [math.log(1.0 - 2.0 ** (-5.0 - h)) for h in range(ret_heads)]

    @pl.when(jnp.logical_and(pl.program_id(0) == 0, t == 0))
    def _():
        i = lax.broadcasted_iota(jnp.int32, (tt, tt), 0)
        j = lax.broadcasted_iota(jnp.int32, (tt, tt), 1)
        for h in range(ret_heads):
            dmat_scr[h] = jnp.where(i >= j, jnp.exp(jnp.where(i >= j, (i - j).astype(F32) * log_gamma[h], 0.0)), 0.0)

    @pl.when(t == 0)
    def _():
        sret_scr[...] = jnp.zeros_like(sret_scr)
        sgla_scr[...] = jnp.zeros_like(sgla_scr)

    x = x_ref[0]
    mod = mod_ref[0]
    sh1, sc1, g1, sh2, sc2 = mod[0:1], mod[1:2], mod[2:3], mod[3:4], mod[4:5]
    h = _rms(x, npre_ref[...]) * (1.0 + sc1) + sh1
    proj_scr[...] = _dot(h.astype(BF16), win_ref[...])

    cos = cos_ref[...]
    sin = sin_ref[...]
    row = lax.broadcasted_iota(jnp.int32, (tt, rdk), 0).astype(F32)
    for hd in range(ret_heads):
        lg = log_gamma[hd]
        q = _rotary(proj_scr[:, rq0 + hd * rdk: rq0 + (hd + 1) * rdk], cos, sin)
        k = _rotary(proj_scr[:, rk0 + hd * rdk: rk0 + (hd + 1) * rdk], cos, sin) * (rdk ** -0.5)
        v = proj_scr[:, rv0 + hd * rdv: rv0 + (hd + 1) * rdv].astype(BF16)
        g = proj_scr[:, rg0 + hd * rdv: rg0 + (hd + 1) * rdv]
        qb = q.astype(BF16)
        s_old = sret_scr[hd]
        scores = (_dot_nt(qb, k.astype(BF16)) * dmat_scr[hd]).astype(BF16)
        o = _dot(scores, v) + jnp.exp((row + 1.0) * lg) * _dot(qb, s_old.astype(BF16))
        k_dec = (k * jnp.exp((tt - 1.0 - row) * lg)).astype(BF16)
        sret_scr[hd] = s_old * math.exp(tt * lg) + _dot_tn(k_dec, v)
        mu = jnp.mean(o, axis=-1, keepdims=True)
        oc = o - mu
        var = jnp.mean(oc * oc, axis=-1, keepdims=True)
        y = oc * lax.rsqrt(var + EPS) * rnw_ref[:, hd * rdv:(hd + 1) * rdv] + rnb_ref[:, hd * rdv:(hd + 1) * rdv]
        mix_scr[:, hd * rdv:(hd + 1) * rdv] = (_silu(g) * y).astype(BF16)

    c = GLA_CHUNK
    n_chunks = tt // c
    gw = gla_heads * gdk
    ga = proj_scr[:, ga0:ga0 + LANES].astype(BF16)
    logit = _dot(ga, gup_ref[...]) + gbias_ref[...]
    la = _log_sigmoid(logit) * (1.0 / GLA_GATE_NORM)
    ri = lax.broadcasted_iota(jnp.int32, (tt, tt), 0)
    ci = lax.broadcasted_iota(jnp.int32, (tt, tt), 1)
    csh = int(math.log2(c))
    tril = jnp.where((ri >= ci) & ((ri >> csh) == (ci >> csh)), 1.0, 0.0).astype(BF16)
    la_hi, la_mid, la_lo = _split3(la)
    bcum = _dot(tril, la_hi) + _dot(tril, la_mid) + _dot(tril, la_lo)
    levels = [s for s in (32, 16, 8, 4, 2, 1, 0) if s < c]
    masks = _gla_level_masks(c, gla_heads, levels)
    hm_k = _head_masks(c, gla_heads, gdk)
    hm_v = _head_masks(c, gla_heads, gdv)
    bd_mask = (lax.broadcasted_iota(jnp.int32, (gla_heads * gdv, gw), 0) >> int(math.log2(gdv))) == (
        lax.broadcasted_iota(jnp.int32, (gla_heads * gdv, gw), 1) >> int(math.log2(gdk)))
    for ch in range(n_chunks):
        r0 = ch * c
        q = proj_scr[r0:r0 + c, gq0:gq0 + gw] * (gdk ** -0.5)
        k = proj_scr[r0:r0 + c, gk0:gk0 + gw]
        v = proj_scr[r0:r0 + c, gv0:gv0 + gla_heads * gdv].astype(BF16)
        bc = bcum[r0:r0 + c]
        scores = _gla_intra_scores(q, k, bc, levels, masks, hm_k)
        o = _dot(scores.astype(BF16), _block_diag_rows(v, hm_v))
        st = sgla_scr[...]
        o = o + _dot_nt((q * jnp.exp(bc)).astype(BF16), st.astype(BF16))
        b_last = bc[c - 1:c, :]
        k_dec = (k * jnp.exp(b_last - bc)).astype(BF16)
        sgla_scr[...] = st * jnp.exp(b_last) + jnp.where(bd_mask, _dot_tn(v, k_dec), 0.0)
        gg = proj_scr[r0:r0 + c, gg0:gg0 + gla_heads * gdv]
        for hd in range(gla_heads):
            oh = o[:, hd * gdv:(hd + 1) * gdv]
            y = oh * lax.rsqrt(jnp.mean(oh * oh, axis=-1, keepdims=True) + EPS) * gnw_ref[:, hd * gdv:(hd + 1) * gdv]
            mix_scr[r0:r0 + c, ret_heads * rdv + hd * gdv: ret_heads * rdv + (hd + 1) * gdv] = (
                _silu(gg[:, hd * gdv:(hd + 1) * gdv]) * y).astype(BF16)

    x1, h2, logits_t = _mixer_tail(x, mix_scr[...], wout_ref[...], g1, npost_ref[...], npre2_ref[...],
                                   sc2, sh2, wrhi_ref[...], wrlo_ref[...])
    x1_ref[...] = x1
    h2_ref[...] = h2
    lg_ref[...] = logits_t
    _store_token_tiles(h3_ref, h2)

    @pl.when(t == pl.num_programs(1) - 1)
    def _():
        sret_ref[0] = sret_scr[...]
        sgla_ref[0] = sgla_scr[...]


def _prompt_mixer(x, mod, p, cos, sin, *, tile, n_total):
    b, t, d = x.shape
    nt = t // tile
    tiles = n_total // tile
    extra = -(-(tiles - b * nt) // nt)
    rh, gh = p["ret_heads"], p["gla_heads"]
    in_w = p["w_in"].shape[1]
    ne = p["wr_hi"].shape[0]
    gw = p["gup"].shape[1]
    full = lambda a: pl.BlockSpec(a.shape, lambda i, j: (0,) * a.ndim)
    vecs = [p["npre"], p["npost"], p["npre2"], p["gbias"], p["rnw"], p["rnb"], p["gnw"]]
    mats = [p["w_in"], p["gup"], p["w_out"], p["wr_hi"], p["wr_lo"]]
    kern = functools.partial(_prompt_mixer_kernel, n_seq=b, ret_heads=rh, gla_heads=gh)
    seq = lambda i: jnp.minimum(i, b - 1)
    out_tile = lambda i, j: jnp.minimum(i * nt + j, tiles - 1)
    return pl.pallas_call(
        kern,
        grid=(b + extra, nt),
        in_specs=[pl.BlockSpec((1, tile, d), lambda i, j: (seq(i), j, 0)),
                  pl.BlockSpec((1, 6, d), lambda i, j: (seq(i), 0, 0))]
                 + [full(a) for a in vecs] + [full(a) for a in mats]
                 + [pl.BlockSpec((tile, LANES), lambda i, j: (j, 0)),
                    pl.BlockSpec((tile, LANES), lambda i, j: (j, 0))],
        out_specs=[pl.BlockSpec((tile, d), lambda i, j: (out_tile(i, j), 0)),
                   pl.BlockSpec((tile, d), lambda i, j: (out_tile(i, j), 0)),
                   pl.BlockSpec((ne, tile), lambda i, j: (0, out_tile(i, j))),
                   pl.BlockSpec((tile * (d // LANES), LANES), lambda i, j: (out_tile(i, j), 0)),
                   pl.BlockSpec((1, rh, 128, 128), lambda i, j: (seq(i), 0, 0, 0)),
                   pl.BlockSpec((1, gh * 128, gw), lambda i, j: (seq(i), 0, 0))],
        out_shape=[jax.ShapeDtypeStruct((n_total, d), F32),
                   jax.ShapeDtypeStruct((n_total, d), F32),
                   jax.ShapeDtypeStruct((ne, n_total), F32),
                   jax.ShapeDtypeStruct((n_total * (d // LANES), LANES), F32),
                   jax.ShapeDtypeStruct((b, rh, 128, 128), F32),
                   jax.ShapeDtypeStruct((b, gh * 128, gw), F32)],
        scratch_shapes=[pltpu.VMEM((tile, in_w), F32),
                        pltpu.VMEM((tile, p["w_out"].shape[0]), BF16),
                        pltpu.VMEM((rh, tile, tile), F32),
                        pltpu.VMEM((rh, 128, 128), F32),
                        pltpu.VMEM((gh * 128, gw), F32)],
        compiler_params=pltpu.CompilerParams(
            dimension_semantics=("arbitrary", "arbitrary"), vmem_limit_bytes=VMEM_LIMIT_BYTES),
        name="prompt_mixer",
    )(x, mod, *vecs, *mats, cos, sin)


def _sample_mixer_kernel(x_ref, mod_ref, npre_ref, npost_ref, npre2_ref, gbias_ref, rnw_ref, rnb_ref,
                         gnw_ref, win_ref, gup_ref, wout_ref, wrhi_ref, wrlo_ref, cos_ref, sin_ref,
                         sret_in_ref, sgla_in_ref, x1_any, h2_any, lg_any, h3_any,
                         x1_ref, h2_ref, lg_ref, h3_ref, sret_ref, sgla_ref,
                         proj_scr, mix_scr, qrot_scr, krot_scr, oret_scr, bc_scr, gqe_scr, ogla_scr,
                         *, ret_heads, gla_heads, ts):
    del x1_any, h2_any, lg_any, h3_any
    rows = x_ref.shape[0]
    nb = rows // ts
    rdk = 128
    rdv = 128
    gdk = gup_ref.shape[1] // gla_heads
    gdv = 128
    gw = gla_heads * gdk
    gvw = gla_heads * gdv
    rq0, rk0, rv0, rg0 = 0, ret_heads * rdk, 2 * ret_heads * rdk, 2 * ret_heads * rdk + ret_heads * rdv
    gq0 = rg0 + ret_heads * rdv
    gk0 = gq0 + gw
    gv0 = gk0 + gw
    gg0 = gv0 + gvw
    ga0 = gg0 + gvw
    log_gamma = [math.log(1.0 - 2.0 ** (-5.0 - h)) for h in range(ret_heads)]
    tsh = int(math.log2(ts))

    def per_row(m):
        return jnp.broadcast_to(m[:, None, :], (nb, ts, m.shape[-1])).reshape(rows, m.shape[-1])

    x = x_ref[...]
    sh1, sc1, g1, sh2, sc2 = (per_row(mod_ref[i]) for i in range(5))
    h = _rms(x, npre_ref[...]) * (1.0 + sc1) + sh1
    proj_scr[...] = _dot(h.astype(BF16), win_ref[...])

    cos = cos_ref[...]
    sin = sin_ref[...]
    ri = lax.broadcasted_iota(jnp.int32, (rows, rows), 0)
    ci = lax.broadcasted_iota(jnp.int32, (rows, rows), 1)
    same = (ri >= ci) & ((ri >> tsh) == (ci >> tsh))
    for hd in range(ret_heads):
        lg = log_gamma[hd]
        q = _rotary(proj_scr[:, rq0 + hd * rdk: rq0 + (hd + 1) * rdk], cos, sin)
        k = _rotary(proj_scr[:, rk0 + hd * rdk: rk0 + (hd + 1) * rdk], cos, sin) * (rdk ** -0.5)
        v = proj_scr[:, rv0 + hd * rdv: rv0 + (hd + 1) * rdv].astype(BF16)
        qrot_scr[:, hd * rdk:(hd + 1) * rdk] = q
        krot_scr[:, hd * rdk:(hd + 1) * rdk] = k
        dmat = jnp.where(same, jnp.exp(jnp.where(same, (ri - ci).astype(F32) * lg, 0.0)), 0.0)
        scores = (_dot_nt(q.astype(BF16), k.astype(BF16)) * dmat).astype(BF16)
        oret_scr[:, hd * rdv:(hd + 1) * rdv] = _dot(scores, v)

    ga = proj_scr[:, ga0:ga0 + LANES].astype(BF16)
    logit = _dot(ga, gup_ref[...]) + gbias_ref[...]
    la = _log_sigmoid(logit) * (1.0 / GLA_GATE_NORM)
    tril = jnp.where(same, 1.0, 0.0).astype(BF16)
    la_hi, la_mid, la_lo = _split3(la)
    bcum = _dot(tril, la_hi) + _dot(tril, la_mid) + _dot(tril, la_lo)
    bc_scr[...] = bcum
    c = min(GLA_CHUNK, rows)
    levels = [s for s in (32, 16, 8, 4, 2, 1, 0) if s < ts]
    masks = _gla_level_masks(c, gla_heads, levels)
    hm_k = _head_masks(c, gla_heads, gdk)
    hm_v = _head_masks(c, gla_heads, gdv)
    for ch in range(rows // c):
        r0 = ch * c
        q = proj_scr[r0:r0 + c, gq0:gq0 + gw] * (gdk ** -0.5)
        k = proj_scr[r0:r0 + c, gk0:gk0 + gw]
        v = proj_scr[r0:r0 + c, gv0:gv0 + gvw].astype(BF16)
        bc = bcum[r0:r0 + c]
        scores = _gla_intra_scores(q, k, bc, levels, masks, hm_k)
        ogla_scr[r0:r0 + c, :] = _dot(scores.astype(BF16), _block_diag_rows(v, hm_v))
        gqe_scr[r0:r0 + c, :] = q * jnp.exp(bc)

    trow = lax.broadcasted_iota(jnp.int32, (ts, rdk), 0).astype(F32)
    eye = lax.broadcasted_iota(jnp.int32, (gw, gw), 0) == lax.broadcasted_iota(jnp.int32, (gw, gw), 1)
    zero_blk = jnp.zeros((gdk, gdv), BF16)

    def element(b, carry):
        rs = pl.ds(pl.multiple_of(b * ts, ts), ts)
        for hd in range(ret_heads):
            lg = log_gamma[hd]
            q = qrot_scr[rs, hd * rdk:(hd + 1) * rdk]
            k = krot_scr[rs, hd * rdk:(hd + 1) * rdk]
            v = proj_scr[rs, rv0 + hd * rdv: rv0 + (hd + 1) * rdv]
            s_old = sret_in_ref[b, hd]
            oret_scr[rs, hd * rdv:(hd + 1) * rdv] += jnp.exp((trow + 1.0) * lg) * _dot(
                q.astype(BF16), s_old.astype(BF16))
            k_dec = (k * jnp.exp((ts - 1.0 - trow) * lg)).astype(BF16)
            sret_ref[b, hd] = s_old * math.exp(ts * lg) + _dot_tn(k_dec, v.astype(BF16))
        s_b = sgla_in_ref[b]
        s_bd = jnp.concatenate(
            [jnp.concatenate([s_b[hd].astype(BF16) if h2 == hd else zero_blk for h2 in range(gla_heads)], axis=1)
             for hd in range(gla_heads)], axis=0)
        ogla_scr[rs, :] += _dot(gqe_scr[rs, :].astype(BF16), s_bd)
        bc = bc_scr[rs, :]
        b_last = bc[ts - 1:ts, :]
        k_dec = (proj_scr[rs, gk0:gk0 + gw] * jnp.exp(b_last - bc)).astype(BF16)
        upd = _dot_tn(k_dec, proj_scr[rs, gv0:gv0 + gvw].astype(BF16))
        decay_col = jnp.sum(jnp.where(eye, jnp.broadcast_to(jnp.exp(b_last), (gw, gw)), 0.0), axis=1, keepdims=True)
        for hd in range(gla_heads):
            sgla_ref[b, hd] = (s_b[hd] * decay_col[hd * gdk:(hd + 1) * gdk]
                               + upd[hd * gdk:(hd + 1) * gdk, hd * gdv:(hd + 1) * gdv])
        return carry

    lax.fori_loop(0, nb, element, 0)

    for hd in range(ret_heads):
        o = oret_scr[:, hd * rdv:(hd + 1) * rdv]
        g = proj_scr[:, rg0 + hd * rdv: rg0 + (hd + 1) * rdv]
        mu = jnp.mean(o, axis=-1, keepdims=True)
        oc = o - mu
        var = jnp.mean(oc * oc, axis=-1, keepdims=True)
        y = oc * lax.rsqrt(var + EPS) * rnw_ref[:, hd * rdv:(hd + 1) * rdv] + rnb_ref[:, hd * rdv:(hd + 1) * rdv]
        mix_scr[:, hd * rdv:(hd + 1) * rdv] = (_silu(g) * y).astype(BF16)
    for hd in range(gla_heads):
        oh = ogla_scr[:, hd * gdv:(hd + 1) * gdv]
        gg = proj_scr[:, gg0 + hd * gdv: gg0 + (hd + 1) * gdv]
        y = oh * lax.rsqrt(jnp.mean(oh * oh, axis=-1, keepdims=True) + EPS) * gnw_ref[:, hd * gdv:(hd + 1) * gdv]
        mix_scr[:, ret_heads * rdv + hd * gdv: ret_heads * rdv + (hd + 1) * gdv] = (_silu(gg) * y).astype(BF16)

    x1, h2, logits_t = _mixer_tail(x, mix_scr[...], wout_ref[...], g1, npost_ref[...], npre2_ref[...],
                                   sc2, sh2, wrhi_ref[...], wrlo_ref[...])
    x1_ref[...] = x1
    h2_ref[...] = h2
    lg_ref[...] = logits_t
    _store_token_tiles(h3_ref, h2)


def _sample_mixer(x, mod_t, p, cos, sin, state_ret, state_gla, x1_buf, h2_buf, lg_buf, h3_buf, *, ts, tile):
    n, d = x.shape
    off = (x1_buf.shape[0] - n) // tile
    nb = tile // ts
    rh, gh = p["ret_heads"], p["gla_heads"]
    in_w = p["w_in"].shape[1]
    ne = p["wr_hi"].shape[0]
    gw = p["gup"].shape[1]
    gdk = gw // gh
    full = lambda a: pl.BlockSpec(a.shape, lambda i: (0,) * a.ndim)
    vecs = [p["npre"], p["npost"], p["npre2"], p["gbias"], p["rnw"], p["rnb"], p["gnw"]]
    mats = [p["w_in"], p["gup"], p["w_out"], p["wr_hi"], p["wr_lo"]]
    kern = functools.partial(_sample_mixer_kernel, ret_heads=rh, gla_heads=gh, ts=ts)
    n_in = 2 + len(vecs) + len(mats) + 4 + 4
    return pl.pallas_call(
        kern,
        grid=(n // tile,),
        in_specs=[pl.BlockSpec((tile, d), lambda i: (i, 0)),
                  pl.BlockSpec((6, nb, d), lambda i: (0, i, 0))]
                 + [full(a) for a in vecs] + [full(a) for a in mats]
                 + [full(cos), full(sin),
                    pl.BlockSpec((nb, rh, 128, 128), lambda i: (i, 0, 0, 0)),
                    pl.BlockSpec((nb, gh, gdk, 128), lambda i: (i, 0, 0, 0)),
                    pl.BlockSpec(memory_space=pl.ANY), pl.BlockSpec(memory_space=pl.ANY),
                    pl.BlockSpec(memory_space=pl.ANY), pl.BlockSpec(memory_space=pl.ANY)],
        out_specs=[pl.BlockSpec((tile, d), lambda i: (off + i, 0)),
                   pl.BlockSpec((tile, d), lambda i: (off + i, 0)),
                   pl.BlockSpec((ne, tile), lambda i: (0, off + i)),
                   pl.BlockSpec((tile * (d // LANES), LANES), lambda i: (off + i, 0)),
                   pl.BlockSpec((nb, rh, 128, 128), lambda i: (i, 0, 0, 0)),
                   pl.BlockSpec((nb, gh, gdk, 128), lambda i: (i, 0, 0, 0))],
        out_shape=[jax.ShapeDtypeStruct(x1_buf.shape, F32),
                   jax.ShapeDtypeStruct(h2_buf.shape, F32),
                   jax.ShapeDtypeStruct(lg_buf.shape, F32),
                   jax.ShapeDtypeStruct(h3_buf.shape, F32),
                   jax.ShapeDtypeStruct(state_ret.shape, F32),
                   jax.ShapeDtypeStruct(state_gla.shape, F32)],
        input_output_aliases={n_in - 4: 0, n_in - 3: 1, n_in - 2: 2, n_in - 1: 3},
        scratch_shapes=[pltpu.VMEM((tile, in_w), F32),
                        pltpu.VMEM((tile, p["w_out"].shape[0]), BF16),
                        pltpu.VMEM((tile, rh * 128), F32),
                        pltpu.VMEM((tile, rh * 128), F32),
                        pltpu.VMEM((tile, rh * 128), F32),
                        pltpu.VMEM((tile, gw), F32),
                        pltpu.VMEM((tile, gw), F32),
                        pltpu.VMEM((tile, gh * 128), F32)],
        compiler_params=pltpu.CompilerParams(
            dimension_semantics=("arbitrary",), vmem_limit_bytes=VMEM_LIMIT_BYTES),
        name="sample_mixer",
    )(x, mod_t, *vecs, *mats, cos, sin, state_ret, state_gla, x1_buf, h2_buf, lg_buf, h3_buf)


def _router_kernel(lg_ref, bias_ref, idx_ref, w_ref, cnt_ref, cnt_scr, *, n_groups, topk_groups, top_k):
    ne, tn = lg_ref.shape
    gsz = ne // n_groups
    neg = -jnp.inf

    @pl.when(pl.program_id(1) == 0)
    def _():
        cnt_scr[...] = jnp.zeros_like(cnt_scr)

    scores = jax.nn.sigmoid(lg_ref[...])
    sel = scores + bias_ref[...][:, 0:1]
    sel3 = sel.reshape(n_groups, gsz, tn)
    mem = lax.broadcasted_iota(jnp.int32, (n_groups, gsz, tn), 1)
    m1 = jnp.max(sel3, axis=1, keepdims=True)
    first = jnp.min(jnp.where(sel3 == m1, mem, gsz), axis=1, keepdims=True)
    m2 = jnp.max(jnp.where(mem == first, neg, sel3), axis=1, keepdims=True)
    gscore = (m1 + m2).reshape(n_groups, tn)
    gi = lax.broadcasted_iota(jnp.int32, (n_groups, tn), 0)
    gsel = jnp.zeros((n_groups, tn), jnp.bool_)
    work = gscore
    for _ in range(topk_groups):
        mx = jnp.max(work, axis=0, keepdims=True)
        pick = gi == jnp.min(jnp.where(work == mx, gi, n_groups), axis=0, keepdims=True)
        gsel = jnp.logical_or(gsel, pick)
        work = jnp.where(pick, neg, work)
    emask = jnp.broadcast_to(gsel[:, None, :], (n_groups, gsz, tn)).reshape(ne, tn)
    ei = lax.broadcasted_iota(jnp.int32, (ne, tn), 0)
    work = jnp.where(emask, sel, neg)
    chosen_any = jnp.zeros((ne, tn), jnp.bool_)
    idx_rows, w_rows = [], []
    for _ in range(top_k):
        mx = jnp.max(work, axis=0, keepdims=True)
        first_e = jnp.min(jnp.where(work == mx, ei, ne), axis=0, keepdims=True)
        pick = ei == first_e
        chosen_any = jnp.logical_or(chosen_any, pick)
        work = jnp.where(pick, neg, work)
        idx_rows.append(first_e)
        w_rows.append(jnp.sum(jnp.where(pick, scores, 0.0), axis=0, keepdims=True))
    wsum = w_rows[0]
    for r in w_rows[1:]:
        wsum = wsum + r
    idx_ref[...] = jnp.concatenate(idx_rows, axis=0)
    w_ref[...] = jnp.concatenate(w_rows, axis=0) / wsum * ROUTED_SCALE
    cnt_scr[...] += _dot(jnp.where(chosen_any, 1.0, 0.0).astype(BF16), jnp.ones((tn, LANES), BF16))
    cnt_ref[0] = cnt_scr[...]


def _router(logits_t, bias, *, parts, tile):
    ne, n = logits_t.shape
    tiles = n // parts // tile
    kern = functools.partial(_router_kernel, n_groups=N_GROUPS, topk_groups=TOPK_GROUPS, top_k=TOP_K)
    return pl.pallas_call(
        kern,
        grid=(parts, tiles),
        in_specs=[pl.BlockSpec((ne, tile), lambda p, j: (0, p * tiles + j)),
                  pl.BlockSpec((ne, LANES), lambda p, j: (0, 0))],
        out_specs=[pl.BlockSpec((TOP_K, tile), lambda p, j: (0, p * tiles + j)),
                   pl.BlockSpec((TOP_K, tile), lambda p, j: (0, p * tiles + j)),
                   pl.BlockSpec((1, ne, LANES), lambda p, j: (p, 0, 0))],
        out_shape=[jax.ShapeDtypeStruct((TOP_K, n), jnp.int32),
                   jax.ShapeDtypeStruct((TOP_K, n), F32),
                   jax.ShapeDtypeStruct((parts, ne, LANES), F32)],
        scratch_shapes=[pltpu.VMEM((ne, LANES), F32)],
        compiler_params=pltpu.CompilerParams(
            dimension_semantics=("arbitrary", "arbitrary"), vmem_limit_bytes=VMEM_LIMIT_BYTES),
        name="router_topk",
    )(logits_t, jnp.broadcast_to(bias.reshape(ne, 1), (ne, LANES)))


def _dispatch_plan(idx_t, w_t, counts, *, parts, blk, rows_per_token):
    k, n = idx_t.shape
    npart = n // parts
    ne = counts.shape[1]
    stride = npart + blk
    big = ne * stride
    local = jnp.arange(n, dtype=jnp.int32) % npart
    to_parts = lambda a: a.reshape(k, parts, npart).transpose(1, 0, 2).reshape(parts, k * npart)
    keys = to_parts(idx_t * stride + local[None, :])
    wts = to_parts(w_t)
    need = (-counts) % blk
    j = jnp.arange(blk - 1, dtype=jnp.int32)
    pad_keys = jnp.where(j[None, None, :] < need[:, :, None],
                         jnp.arange(ne, dtype=jnp.int32)[None, :, None] * stride + npart + j[None, None, :], big)
    total = k * npart + ne * (blk - 1)
    nblk = -(-total // blk)
    fill = nblk * blk - total
    all_keys = jnp.concatenate([keys, pad_keys.reshape(parts, -1), jnp.full((parts, fill), big, jnp.int32)], axis=1)
    all_w = jnp.concatenate([wts, jnp.zeros((parts, nblk * blk - k * npart), F32)], axis=1)
    sk, sw = lax.sort((all_keys, all_w), dimension=1, num_keys=1)
    valid = sk < big
    slot_tok = sk % stride
    real = jnp.logical_and(valid, slot_tok < npart)
    acc_row = jnp.where(real, slot_tok, npart) * rows_per_token
    src_row = jnp.where(real, slot_tok, 0) * rows_per_token
    w = jnp.where(real, sw, 0.0)
    blocks_per_expert = (counts + blk - 1) // blk
    first_blk = jnp.concatenate([jnp.zeros((parts, 1), jnp.int32),
                                 jnp.cumsum(blocks_per_expert, axis=1, dtype=jnp.int32)], axis=1)
    pad_src = jnp.zeros((parts, 2 * blk), jnp.int32)
    pad_acc = jnp.full((parts, 2 * blk), npart * rows_per_token, jnp.int32)
    pad_w = jnp.zeros((parts, 2 * blk), F32)
    src_t = jnp.concatenate([src_row, pad_src], axis=1)
    acc_t = jnp.concatenate([pad_acc, acc_row], axis=1)
    w_t2 = jnp.concatenate([pad_w, w], axis=1)
    return src_t.reshape(-1), acc_t.reshape(-1), w_t2.reshape(-1), first_blk, nblk


def _moe_kernel(first_ref, src_hbm, dst_hbm, wts_hbm, h3_ref, h2_ref, x1_ref, g2_ref, npost_ref,
                wg_ref, wu_ref, wd_ref, wsg_ref, wsu_ref, wsd_ref, ya_ref, yb_ref,
                acc_scr, g_scr, y2_scr, xs_scr, wgu_scr, wdn_scr, src_smem, dst_smem, wts_smem, sem,
                *, ne, nblk, blk, ts, n_epi, tiles_a):
    p = pl.program_id(0)
    b = pl.program_id(1)
    d = h2_ref.shape[1]
    ff = wg_ref.shape[2]
    nchunk = d // LANES
    tile_at = lambda row: pl.ds(pl.multiple_of(row, nchunk), nchunk)

    depth = src_smem.shape[0]
    n_valid = first_ref[p, ne]
    last_t = n_valid + 1

    def table_copies(t):
        slot = t & (depth - 1)
        row = pl.multiple_of((p * (nblk + 2) + t) * blk, blk)
        return tuple(pltpu.make_async_copy(hbm.at[pl.ds(row, blk)], smem.at[slot], sem.at[i, slot])
                     for i, (hbm, smem) in enumerate(((src_hbm, src_smem), (dst_hbm, dst_smem), (wts_hbm, wts_smem))))

    def gather(slot, xs_ref):
        for r in range(blk):
            g_scr[r * nchunk:(r + 1) * nchunk, :] = h3_ref[tile_at(src_smem[slot, r]), :]
        for c in range(nchunk):
            xs_ref[:, c * LANES:(c + 1) * LANES] = g_scr[pl.ds(c, blk, stride=nchunk), :].astype(BF16)

    def expert_ffn(xs_ref, y2_ref):
        gu = _dot(xs_ref[...], wgu_scr[...])
        hid = (_silu(gu[:, :ff]) * gu[:, ff:]).astype(BF16)
        y = _dot(hid, wdn_scr[...])
        for c in range(nchunk):
            y2_ref[pl.ds(c, blk, stride=nchunk), :] = y[:, c * LANES:(c + 1) * LANES]

    def scatter_add(slot, y2_ref):
        for g in range(blk // SUBLANES):
            rows = [g * SUBLANES + i for i in range(SUBLANES)]
            dsts = [dst_smem[slot, r] for r in rows]
            new = [acc_scr[tile_at(t), :] + wts_smem[slot, r] * y2_ref[r * nchunk:(r + 1) * nchunk, :]
                   for t, r in zip(dsts, rows)]
            for t, v in zip(dsts, new):
                acc_scr[tile_at(t), :] = v

    @pl.when(b == 0)
    def _():
        acc_scr[...] = jnp.zeros_like(acc_scr)
        y2_scr[...] = jnp.zeros_like(y2_scr)
        for t in range(depth):
            @pl.when(t <= last_t)
            def _():
                for cp in table_copies(t):
                    cp.start()
        for cp in table_copies(0):
            cp.wait()
        gather(0, xs_scr.at[0])

    def pipeline_step(f, carry):
        t = f + 1
        slot = t & (depth - 1)
        for cp in table_copies(t):
            cp.wait()

        @pl.when(t + (depth - 1) <= last_t)
        def _():
            for cp in table_copies(t + (depth - 1)):
                cp.start()

        for parity in range(2):
            @pl.when((t & 1) == parity)
            def _():
                gather(slot, xs_scr.at[parity])
                expert_ffn(xs_scr.at[1 - parity], y2_scr.at[1 - parity])
                scatter_add(slot, y2_scr.at[parity])
        return carry

    @pl.when(b < ne)
    def _():
        e = jnp.minimum(b, ne - 1)
        f0, f1 = first_ref[p, e], first_ref[p, e + 1]

        @pl.when(f1 > f0)
        def _():
            wgu_scr[:, :ff] = wg_ref[0].astype(BF16)
            wgu_scr[:, ff:] = wu_ref[0].astype(BF16)
            wdn_scr[...] = wd_ref[0].astype(BF16)
            lax.fori_loop(f0, f1, pipeline_step, 0)

    @pl.when(b == ne)
    def _():
        for cp in table_copies(last_t):
            cp.wait()
        for parity in range(2):
            @pl.when((last_t & 1) == parity)
            def _():
                scatter_add(last_t & (depth - 1), y2_scr.at[parity])

    @pl.when(b >= ne)
    def _():
        rows = ya_ref.shape[0]
        t0 = pl.multiple_of((b - ne) * rows, rows)
        routed = jnp.concatenate(
            [acc_scr[pl.ds(t0 * nchunk + c, rows, stride=nchunk), :] for c in range(nchunk)], axis=1)
        hs = h2_ref[...].astype(BF16)
        hid = (_silu(_dot(hs, wsg_ref[...])) * _dot(hs, wsu_ref[...])).astype(BF16)
        f = routed + _dot(hid, wsd_ref[...])
        g2 = g2_ref[0]
        g2 = jnp.broadcast_to(g2[:, None, :], (rows // ts, ts, d)).reshape(rows, d)
        y = x1_ref[...] + g2 * _rms(f, npost_ref[...])
        tile = p * n_epi + (b - ne)

        @pl.when(tile < tiles_a)
        def _():
            ya_ref[...] = y

        @pl.when(tile >= tiles_a)
        def _():
            yb_ref[...] = y


def _moe(h2, h3, x1, g2_tiles, npost, plan, w_gate, w_up, w_down, wsg, wsu, wsd, *, parts, blk, ts, n_a):
    src_row, acc_row, wts, first_blk, nblk = plan
    n, d = h2.shape
    npart = n // parts
    ne, _, ff = w_gate.shape
    epi = LANES
    n_epi = npart // epi
    tiles_a = n_a // epi
    tiles_b = (n - n_a) // epi
    nchunk = d // LANES
    kern = functools.partial(_moe_kernel, ne=ne, nblk=nblk, blk=blk, ts=ts, n_epi=n_epi, tiles_a=tiles_a)
    epi_of = lambda p, b: p * n_epi + jnp.clip(b - ne, 0, n_epi - 1)
    exp_of = lambda p, b: jnp.minimum(b, ne - 1)
    full = lambda a: pl.BlockSpec(a.shape, lambda p, b, fb: (0,) * a.ndim)
    grid_spec = pltpu.PrefetchScalarGridSpec(
        num_scalar_prefetch=1,
        grid=(parts, ne + n_epi),
        in_specs=[
            pl.BlockSpec(memory_space=pl.ANY),
            pl.BlockSpec(memory_space=pl.ANY),
            pl.BlockSpec(memory_space=pl.ANY),
            pl.BlockSpec((npart * nchunk, LANES), lambda p, b, fb: (p, 0), pipeline_mode=pl.Buffered(1)),
            pl.BlockSpec((epi, d), lambda p, b, fb: (epi_of(p, b), 0)),
            pl.BlockSpec((epi, d), lambda p, b, fb: (epi_of(p, b), 0)),
            pl.BlockSpec((1, epi // ts, d), lambda p, b, fb: (epi_of(p, b), 0, 0)),
            full(npost),
            pl.BlockSpec((1, d, ff), lambda p, b, fb: (exp_of(p, b), 0, 0)),
            pl.BlockSpec((1, d, ff), lambda p, b, fb: (exp_of(p, b), 0, 0)),
            pl.BlockSpec((1, ff, d), lambda p, b, fb: (exp_of(p, b), 0, 0)),
            full(wsg), full(wsu), full(wsd),
        ],
        out_specs=[
            pl.BlockSpec((epi, d), lambda p, b, fb: (jnp.minimum(epi_of(p, b), tiles_a - 1), 0)),
            pl.BlockSpec((epi, d), lambda p, b, fb: (jnp.clip(epi_of(p, b) - tiles_a, 0, tiles_b - 1), 0))],
        scratch_shapes=[pltpu.VMEM(((npart + 1) * (d // LANES), LANES), F32),
                        pltpu.VMEM((blk * (d // LANES), LANES), F32),
                        pltpu.VMEM((2, blk * (d // LANES), LANES), F32),
                        pltpu.VMEM((2, blk, d), BF16),
                        pltpu.VMEM((d, 2 * ff), BF16),
                        pltpu.VMEM((ff, d), BF16),
                        pltpu.SMEM((MOE_TABLE_DEPTH, blk), jnp.int32),
                        pltpu.SMEM((MOE_TABLE_DEPTH, blk), jnp.int32),
                        pltpu.SMEM((MOE_TABLE_DEPTH, blk), F32),
                        pltpu.SemaphoreType.DMA((3, MOE_TABLE_DEPTH))],
    )
    return pl.pallas_call(
        kern,
        grid_spec=grid_spec,
        out_shape=[jax.ShapeDtypeStruct((n_a, d), F32), jax.ShapeDtypeStruct((n - n_a, d), F32)],
        compiler_params=pltpu.CompilerParams(
            dimension_semantics=("arbitrary", "arbitrary"), vmem_limit_bytes=VMEM_LIMIT_BYTES),
        name="moe_experts",
    )(first_blk, src_row, acc_row, wts, h3, h2, x1, g2_tiles, npost, w_gate, w_up, w_down, wsg, wsu, wsd)


def _rope_tables(pos):
    half = 64
    inv_freq = ROPE_BASE ** (-jnp.arange(half, dtype=F32) / half)
    ang = pos[:, None] * inv_freq[None, :]
    cos, sin = jnp.cos(ang), jnp.sin(ang)
    return jnp.concatenate([cos, cos], axis=1), jnp.concatenate([-sin, sin], axis=1)


def _mixer_params(norm_pre_mix, norm_post_mix, norm_pre_ffn, w_in, gla_gate_up, gla_gate_bias,
                  ret_norm_w, ret_norm_b, gla_norm_w, w_out, w_router):
    d, in_w = w_in.shape
    pad = (-in_w) % LANES
    rank = gla_gate_up.shape[0]
    wr_hi, wr_lo = _split2(w_router.T)
    return dict(
        ret_heads=ret_norm_w.shape[0], gla_heads=gla_norm_w.shape[0],
        npre=norm_pre_mix.reshape(1, d), npost=norm_post_mix.reshape(1, d), npre2=norm_pre_ffn.reshape(1, d),
        gbias=gla_gate_bias.reshape(1, -1), rnw=ret_norm_w.reshape(1, -1), rnb=ret_norm_b.reshape(1, -1),
        gnw=gla_norm_w.reshape(1, -1),
        w_in=jnp.pad(w_in, ((0, 0), (0, pad))).astype(BF16),
        gup=jnp.pad(gla_gate_up, ((0, LANES - rank), (0, 0))).astype(BF16),
        w_out=w_out.astype(BF16), wr_hi=wr_hi, wr_lo=wr_lo)


def kernel(x_prompt, x_sample, state_ret, state_gla, c_prompt, c_sample, w_ada, b_ada, norm_pre_mix, norm_post_mix, norm_pre_ffn, norm_post_ffn, w_in, gla_gate_up, gla_gate_bias, ret_norm_w, ret_norm_b, gla_norm_w, w_out, w_router, router_bias, w_exp_gate, w_exp_up, w_exp_down, w_sh_gate, w_sh_up, w_sh_down):
    bp, tp, d = x_prompt.shape
    bs, ts, _ = x_sample.shape
    l = 0
    c_all = jnp.concatenate([c_prompt, c_sample], axis=0)
    mod = _ada(c_all, w_ada[l], b_ada[l]).reshape(bp + bs, 6, d)
    p = _mixer_params(norm_pre_mix[l], norm_post_mix[l], norm_pre_ffn[l], w_in[l], gla_gate_up[l],
                      gla_gate_bias[l], ret_norm_w[l], ret_norm_b[l], gla_norm_w[l], w_out[l], w_router[l])
    n_p, n_s = bp * tp, bs * ts
    n = n_p + n_s
    cos_p, sin_p = _rope_tables(jnp.arange(tp, dtype=F32))
    x1, h2, logits_t, h3, sret_p, sgla_p = _prompt_mixer(x_prompt, mod[:bp], p, cos_p, sin_p,
                                                         tile=PROMPT_TILE, n_total=n)
    cos_s, sin_s = _rope_tables(PAST_LEN + jnp.arange(ts, dtype=F32))
    cos_s, sin_s = jnp.tile(cos_s, (SAMPLE_TILE // ts, 1)), jnp.tile(sin_s, (SAMPLE_TILE // ts, 1))
    x1, h2, logits_t, h3, sret_s, sgla_s = _sample_mixer(
        x_sample.reshape(n_s, d), mod[bp:].transpose(1, 0, 2), p, cos_s, sin_s, state_ret[l], state_gla[l],
        x1, h2, logits_t, h3, ts=ts, tile=SAMPLE_TILE)
    gh, gdk = p["gla_heads"], p["gup"].shape[1] // p["gla_heads"]
    sgla_blocks = sgla_p.reshape(bp, gh, 128, gh, gdk)
    new_gla_p = jnp.stack([sgla_blocks[:, h, :, h, :] for h in range(gh)], axis=1).transpose(0, 1, 3, 2)

    idx_t, w_t, cnt = _router(logits_t, router_bias[l], parts=MOE_PARTS, tile=ROUTER_TILE)
    plan = _dispatch_plan(idx_t, w_t, cnt[:, :, 0].astype(jnp.int32), parts=MOE_PARTS, blk=MOE_BLOCK,
                          rows_per_token=d // LANES)
    per_tile = LANES // ts
    g2 = mod[:, 5, :]
    g2_tiles = jnp.concatenate(
        [jnp.broadcast_to(jnp.repeat(g2[:bp], tp // LANES, axis=0)[:, None, :], (n_p // LANES, per_tile, d)),
         g2[bp:].reshape(n_s // LANES, per_tile, d)], axis=0)
    y_p, y_s = _moe(h2, h3, x1, g2_tiles, norm_post_ffn[l].reshape(1, d), plan,
                    w_exp_gate[l], w_exp_up[l], w_exp_down[l],
                    w_sh_gate[l].astype(BF16), w_sh_up[l].astype(BF16), w_sh_down[l].astype(BF16),
                    parts=MOE_PARTS, blk=MOE_BLOCK, ts=ts, n_a=n_p)
    return (y_p.reshape(bp, tp, d), y_s.reshape(bs, ts, d), sret_p[None], new_gla_p[None],
            sret_s[None], sgla_s[None])
```

```python
import functools
import math

import jax
import jax.numpy as jnp
from jax import lax
from jax.experimental import pallas as pl
from jax.experimental.pallas import tpu as pltpu

F32 = jnp.float32
BF16 = jnp.bfloat16

PAST_LEN = 16384
ROPE_BASE = 10000.0
GLA_GATE_NORM = 16.0
TOP_K = 8
N_GROUPS = 8
TOPK_GROUPS = 4
ROUTED_SCALE = 2.5
EPS = 1e-6

LANES = 128
SUBLANES = 8
VMEM_LIMIT_BYTES = 56 * 1024 * 1024
MOE_VMEM_LIMIT_BYTES = 60 * 1024 * 1024

GLA_CHUNK = 64
PROMPT_TILE = 256
SAMPLE_TILE = 128
ROUTER_TILE = 256
MOE_PARTS = 4
MOE_BLOCK = 256
MOE_EPILOGUE_TILE = 256
MOE_TABLE_DEPTH = 8


def _dot(a, b):
    return jnp.dot(a, b, preferred_element_type=F32)


def _dot_nt(a, b):
    return lax.dot_general(a, b, (((1,), (1,)), ((), ())), preferred_element_type=F32)


def _dot_tn(a, b):
    return lax.dot_general(a, b, (((0,), (0,)), ((), ())), preferred_element_type=F32)


def _split2(x):
    hi = x.astype(BF16)
    lo = (x - hi.astype(F32)).astype(BF16)
    return hi, lo


def _split3(x):
    hi = x.astype(BF16)
    r = x - hi.astype(F32)
    mid = r.astype(BF16)
    lo = (r - mid.astype(F32)).astype(BF16)
    return hi, mid, lo


def _rms(x, g):
    return x * lax.rsqrt(jnp.mean(x * x, axis=-1, keepdims=True) + EPS) * g


def _silu(x):
    return x * jax.nn.sigmoid(x)


def _log_sigmoid(x):
    return jnp.minimum(x, 0.0) - jnp.log(1.0 + jnp.exp(-jnp.abs(x)))


def _ada_kernel(c_ref, w_ref, b_ref, o_ref):
    a_hi, a_lo = _split2(_silu(c_ref[...]))
    w_hi, w_lo = _split2(w_ref[...])
    o_ref[...] = _dot(a_hi, w_hi) + _dot(a_hi, w_lo) + _dot(a_lo, w_hi) + b_ref[...]


def _ada(c, w, b):
    rows, d = c.shape
    n = w.shape[1]
    tn = 1024
    return pl.pallas_call(
        _ada_kernel,
        grid=(n // tn,),
        in_specs=[
            pl.BlockSpec((rows, d), lambda j: (0, 0)),
            pl.BlockSpec((d, tn), lambda j: (0, j)),
            pl.BlockSpec((1, tn), lambda j: (0, j)),
        ],
        out_specs=pl.BlockSpec((rows, tn), lambda j: (0, j)),
        out_shape=jax.ShapeDtypeStruct((rows, n), F32),
        compiler_params=pltpu.CompilerParams(
            dimension_semantics=("arbitrary",), vmem_limit_bytes=VMEM_LIMIT_BYTES),
        name="ada_mod",
    )(c, w, b.reshape(1, n))


def _rotary(x, cos, sin_signed):
    return x * cos + pltpu.roll(x, x.shape[-1] // 2, axis=1) * sin_signed


def _level_reference(bc, s):
    rows, width = bc.shape
    pieces = []
    sub = lax.broadcasted_iota(jnp.int32, (SUBLANES, width), 0)
    for g in range(rows // SUBLANES):
        base = g * SUBLANES
        bounds = sorted({((base + r) // (2 * s)) * (2 * s) + s - 1 for r in range(SUBLANES)})
        piece = jnp.broadcast_to(bc[bounds[-1]:bounds[-1] + 1, :], (SUBLANES, width))
        for bm in reversed(bounds[:-1]):
            last_row_of_pair = bm + s - base
            piece = jnp.where(sub <= last_row_of_pair,
                              jnp.broadcast_to(bc[bm:bm + 1, :], (SUBLANES, width)), piece)
        pieces.append(piece)
    return jnp.concatenate(pieces, axis=0)


def _gla_level_masks(rows, heads, levels):
    i = lax.broadcasted_iota(jnp.int32, (rows, heads * rows), 0)
    j = lax.broadcasted_iota(jnp.int32, (rows, heads * rows), 1) & (rows - 1)
    masks = {}
    for s in levels:
        if s == 0:
            masks[s] = i == j
        else:
            sh = int(math.log2(s))
            masks[s] = ((i >> sh) == (j >> sh) + 1) & ((i >> (sh + 1)) == (j >> (sh + 1)))
    return masks


def _gla_intra_scores(q, k, bc, levels, masks, head_masks):
    scores = None
    for s in levels:
        if s == 0:
            qt, kt = q, k
        else:
            ref = _level_reference(bc, s)
            qt = q * jnp.exp(jnp.minimum(bc - ref, 0.0))
            kt = k * jnp.exp(jnp.minimum(ref - bc, 0.0))
        rhs_t = _block_diag_rows(kt.astype(BF16), head_masks)
        sc = jnp.where(masks[s], _dot_nt(qt.astype(BF16), rhs_t), 0.0)
        scores = sc if scores is None else scores + sc
    return scores


def _block_diag_rows(x, col_masks):
    return jnp.concatenate([x * m for m in col_masks], axis=0)


def _head_masks(rows, heads, width):
    lane = lax.broadcasted_iota(jnp.int32, (rows, heads * width), 1)
    sh = int(math.log2(width))
    return [jnp.where((lane >> sh) == h, 1.0, 0.0).astype(BF16) for h in range(heads)]


def _mixer_tail(x, mix_bf16, w_out, g1, npost, npre2, sc2, sh2, wr_hi, wr_lo):
    mix = _dot(mix_bf16, w_out)
    x1 = x + g1 * _rms(mix, npost)
    h2 = _rms(x1, npre2) * (1.0 + sc2) + sh2
    h_hi, h_lo = _split2(h2)
    logits_t = _dot_nt(wr_hi, h_hi) + _dot_nt(wr_hi, h_lo) + _dot_nt(wr_lo, h_hi)
    return x1, h2, logits_t


def _prompt_mixer_kernel(*refs, n_seq, ret_heads, gla_heads):
    @pl.when(pl.program_id(0) < n_seq)
    def _():
        _prompt_mixer_body(*refs, ret_heads=ret_heads, gla_heads=gla_heads)

    @pl.when(pl.program_id(0) >= n_seq)
    def _():
        for out_ref in refs[16:20]:
            out_ref[...] = jnp.zeros_like(out_ref)


def _store_token_tiles(h3_ref, h2):
    rows, d = h2.shape
    nchunk = d // LANES
    for c in range(nchunk):
        h3_ref[pl.ds(c, rows, stride=nchunk), :] = h2[:, c * LANES:(c + 1) * LANES]


def _prompt_mixer_body(x_ref, mod_ref, npre_ref, npost_ref, npre2_ref, gbias_ref, rnw_ref, rnb_ref,
                       gnw_ref, win_ref, gup_ref, wout_ref, wrhi_ref, wrlo_ref, cos_ref, sin_ref,
                       x1_ref, h2_ref, lg_ref, h3_ref, sret_ref, sgla_ref,
                       proj_scr, mix_scr, dmat_scr, sret_scr, sgla_scr, *, ret_heads, gla_heads):
    t = pl.program_id(1)
    tt = x_ref.shape[1]
    rdk = 128
    rdv = 128
    gdk = gup_ref.shape[1] // gla_heads
    gdv = 128
    rq0, rk0, rv0, rg0 = 0, ret_heads * rdk, 2 * ret_heads * rdk, 2 * ret_heads * rdk + ret_heads * rdv
    gq0 = rg0 + ret_heads * rdv
    gk0 = gq0 + gla_heads * gdk
    gv0 = gk0 + gla_heads * gdk
    gg0 = gv0 + gla_heads * gdv
    ga0 = gg0 + gla_heads * gdv
    log_gamma = [math.log(1.0 - 2.0 ** (-5.0 - h)) for h in range(ret_heads)]

    @pl.when(jnp.logical_and(pl.program_id(0) == 0, t == 0))
    def _():
        i = lax.broadcasted_iota(jnp.int32, (tt, tt), 0)
        j = lax.broadcasted_iota(jnp.int32, (tt, tt), 1)
        for h in range(ret_heads):
            dmat_scr[h] = jnp.where(i >= j, jnp.exp(jnp.where(i >= j, (i - j).astype(F32) * log_gamma[h], 0.0)), 0.0)

    @pl.when(t == 0)
    def _():
        sret_scr[...] = jnp.zeros_like(sret_scr)
        sgla_scr[...] = jnp.zeros_like(sgla_scr)

    x = x_ref[0]
    mod = mod_ref[0]
    sh1, sc1, g1, sh2, sc2 = mod[0:1], mod[1:2], mod[2:3], mod[3:4], mod[4:5]
    h = _rms(x, npre_ref[...]) * (1.0 + sc1) + sh1
    proj_scr[...] = _dot(h.astype(BF16), win_ref[...])

    cos = cos_ref[...]
    sin = sin_ref[...]
    row = lax.broadcasted_iota(jnp.int32, (tt, rdk), 0).astype(F32)
    for hd in range(ret_heads):
        lg = log_gamma[hd]
        q = _rotary(proj_scr[:, rq0 + hd * rdk: rq0 + (hd + 1) * rdk], cos, sin)
        k = _rotary(proj_scr[:, rk0 + hd * rdk: rk0 + (hd + 1) * rdk], cos, sin) * (rdk ** -0.5)
        v = proj_scr[:, rv0 + hd * rdv: rv0 + (hd + 1) * rdv].astype(BF16)
        g = proj_scr[:, rg0 + hd * rdv: rg0 + (hd + 1) * rdv]
        qb = q.astype(BF16)
        s_old = sret_scr[hd]
        scores = (_dot_nt(qb, k.astype(BF16)) * dmat_scr[hd]).astype(BF16)
        o = _dot(scores, v) + jnp.exp((row + 1.0) * lg) * _dot(qb, s_old.astype(BF16))
        k_dec = (k * jnp.exp((tt - 1.0 - row) * lg)).astype(BF16)
        sret_scr[hd] = s_old * math.exp(tt * lg) + _dot_tn(k_dec, v)
        mu = jnp.mean(o, axis=-1, keepdims=True)
        oc = o - mu
        var = jnp.mean(oc * oc, axis=-1, keepdims=True)
        y = oc * lax.rsqrt(var + EPS) * rnw_ref[:, hd * rdv:(hd + 1) * rdv] + rnb_ref[:, hd * rdv:(hd + 1) * rdv]
        mix_scr[:, hd * rdv:(hd + 1) * rdv] = (_silu(g) * y).astype(BF16)

    c = GLA_CHUNK
    n_chunks = tt // c
    gw = gla_heads * gdk
    ga = proj_scr[:, ga0:ga0 + LANES].astype(BF16)
    logit = _dot(ga, gup_ref[...]) + gbias_ref[...]
    la = _log_sigmoid(logit) * (1.0 / GLA_GATE_NORM)
    ri = lax.broadcasted_iota(jnp.int32, (tt, tt), 0)
    ci = lax.broadcasted_iota(jnp.int32, (tt, tt), 1)
    csh = int(math.log2(c))
    tril = jnp.where((ri >= ci) & ((ri >> csh) == (ci >> csh)), 1.0, 0.0).astype(BF16)
    la_hi, la_mid, la_lo = _split3(la)
    bcum = _dot(tril, la_hi) + _dot(tril, la_mid) + _dot(tril, la_lo)
    levels = [s for s in (32, 16, 8, 4, 2, 1, 0) if s < c]
    masks = _gla_level_masks(c, gla_heads, levels)
    hm_k = _head_masks(c, gla_heads, gdk)
    hm_v = _head_masks(c, gla_heads, gdv)
    bd_mask = (lax.broadcasted_iota(jnp.int32, (gla_heads * gdv, gw), 0) >> int(math.log2(gdv))) == (
        lax.broadcasted_iota(jnp.int32, (gla_heads * gdv, gw), 1) >> int(math.log2(gdk)))
    for ch in range(n_chunks):
        r0 = ch * c
        q = proj_scr[r0:r0 + c, gq0:gq0 + gw] * (gdk ** -0.5)
        k = proj_scr[r0:r0 + c, gk0:gk0 + gw]
        v = proj_scr[r0:r0 + c, gv0:gv0 + gla_heads * gdv].astype(BF16)
        bc = bcum[r0:r0 + c]
        scores = _gla_intra_scores(q, k, bc, levels, masks, hm_k)
        o = _dot(scores.astype(BF16), _block_diag_rows(v, hm_v))
        st = sgla_scr[...]
        o = o + _dot_nt((q * jnp.exp(bc)).astype(BF16), st.astype(BF16))
        b_last = bc[c - 1:c, :]
        k_dec = (k * jnp.exp(b_last - bc)).astype(BF16)
        sgla_scr[...] = st * jnp.exp(b_last) + jnp.where(bd_mask, _dot_tn(v, k_dec), 0.0)
        gg = proj_scr[r0:r0 + c, gg0:gg0 + gla_heads * gdv]
        for hd in range(gla_heads):
            oh = o[:, hd * gdv:(hd + 1) * gdv]
            y = oh * lax.rsqrt(jnp.mean(oh * oh, axis=-1, keepdims=True) + EPS) * gnw_ref[:, hd * gdv:(hd + 1) * gdv]
            mix_scr[r0:r0 + c, ret_heads * rdv + hd * gdv: ret_heads * rdv + (hd + 1) * gdv] = (
                _silu(gg[:, hd * gdv:(hd + 1) * gdv]) * y).astype(BF16)

    x1, h2, logits_t = _mixer_tail(x, mix_scr[...], wout_ref[...], g1, npost_ref[...], npre2_ref[...],
                                   sc2, sh2, wrhi_ref[...], wrlo_ref[...])
    x1_ref[...] = x1
    h2_ref[...] = h2
    lg_ref[...] = logits_t
    _store_token_tiles(h3_ref, h2)

    @pl.when(t == pl.num_programs(1) - 1)
    def _():
        sret_ref[0] = sret_scr[...]
        sgla_ref[0] = sgla_scr[...]


def _prompt_mixer(x, mod, p, cos, sin, *, tile, n_total):
    b, t, d = x.shape
    nt = t // tile
    tiles = n_total // tile
    extra = -(-(tiles - b * nt) // nt)
    rh, gh = p["ret_heads"], p["gla_heads"]
    in_w = p["w_in"].shape[1]
    ne = p["wr_hi"].shape[0]
    gw = p["gup"].shape[1]
    full = lambda a: pl.BlockSpec(a.shape, lambda i, j: (0,) * a.ndim)
    vecs = [p["npre"], p["npost"], p["npre2"], p["gbias"], p["rnw"], p["rnb"], p["gnw"]]
    mats = [p["w_in"], p["gup"], p["w_out"], p["wr_hi"], p["wr_lo"]]
    kern = functools.partial(_prompt_mixer_kernel, n_seq=b, ret_heads=rh, gla_heads=gh)
    seq = lambda i: jnp.minimum(i, b - 1)
    out_tile = lambda i, j: jnp.minimum(i * nt + j, tiles - 1)
    return pl.pallas_call(
        kern,
        grid=(b + extra, nt),
        in_specs=[pl.BlockSpec((1, tile, d), lambda i, j: (seq(i), j, 0)),
                  pl.BlockSpec((1, 6, d), lambda i, j: (seq(i), 0, 0))]
                 + [full(a) for a in vecs] + [full(a) for a in mats]
                 + [pl.BlockSpec((tile, LANES), lambda i, j: (j, 0)),
                    pl.BlockSpec((tile, LANES), lambda i, j: (j, 0))],
        out_specs=[pl.BlockSpec((tile, d), lambda i, j: (out_tile(i, j), 0)),
                   pl.BlockSpec((tile, d), lambda i, j: (out_tile(i, j), 0)),
                   pl.BlockSpec((ne, tile), lambda i, j: (0, out_tile(i, j))),
                   pl.BlockSpec((tile * (d // LANES), LANES), lambda i, j: (out_tile(i, j), 0)),
                   pl.BlockSpec((1, rh, 128, 128), lambda i, j: (seq(i), 0, 0, 0)),
                   pl.BlockSpec((1, gh * 128, gw), lambda i, j: (seq(i), 0, 0))],
        out_shape=[jax.ShapeDtypeStruct((n_total, d), F32),
                   jax.ShapeDtypeStruct((n_total, d), F32),
                   jax.ShapeDtypeStruct((ne, n_total), F32),
                   jax.ShapeDtypeStruct((n_total * (d // LANES), LANES), F32),
                   jax.ShapeDtypeStruct((b, rh, 128, 128), F32),
                   jax.ShapeDtypeStruct((b, gh * 128, gw), F32)],
        scratch_shapes=[pltpu.VMEM((tile, in_w), F32),
                        pltpu.VMEM((tile, p["w_out"].shape[0]), BF16),
                        pltpu.VMEM((rh, tile, tile), F32),
                        pltpu.VMEM((rh, 128, 128), F32),
                        pltpu.VMEM((gh * 128, gw), F32)],
        compiler_params=pltpu.CompilerParams(
            dimension_semantics=("arbitrary", "arbitrary"), vmem_limit_bytes=VMEM_LIMIT_BYTES),
        name="prompt_mixer",
    )(x, mod, *vecs, *mats, cos, sin)


def _sample_mixer_kernel(x_ref, mod_ref, npre_ref, npost_ref, npre2_ref, gbias_ref, rnw_ref, rnb_ref,
                         gnw_ref, win_ref, gup_ref, wout_ref, wrhi_ref, wrlo_ref, cos_ref, sin_ref,
                         sret_in_ref, sgla_in_ref, x1_any, h2_any, lg_any, h3_any,
                         x1_ref, h2_ref, lg_ref, h3_ref, sret_ref, sgla_ref,
                         proj_scr, mix_scr, qrot_scr, krot_scr, oret_scr, bc_scr, gqe_scr, ogla_scr,
                         *, ret_heads, gla_heads, ts):
    del x1_any, h2_any, lg_any, h3_any
    rows = x_ref.shape[0]
    nb = rows // ts
    rdk = 128
    rdv = 128
    gdk = gup_ref.shape[1] // gla_heads
    gdv = 128
    gw = gla_heads * gdk
    gvw = gla_heads * gdv
    rq0, rk0, rv0, rg0 = 0, ret_heads * rdk, 2 * ret_heads * rdk, 2 * ret_heads * rdk + ret_heads * rdv
    gq0 = rg0 + ret_heads * rdv
    gk0 = gq0 + gw
    gv0 = gk0 + gw
    gg0 = gv0 + gvw
    ga0 = gg0 + gvw
    log_gamma = [math.log(1.0 - 2.0 ** (-5.0 - h)) for h in range(ret_heads)]
    tsh = int(math.log2(ts))

    def per_row(m):
        return jnp.broadcast_to(m[:, None, :], (nb, ts, m.shape[-1])).reshape(rows, m.shape[-1])

    x = x_ref[...]
    sh1, sc1, g1, sh2, sc2 = (per_row(mod_ref[i]) for i in range(5))
    h = _rms(x, npre_ref[...]) * (1.0 + sc1) + sh1
    proj_scr[...] = _dot(h.astype(BF16), win_ref[...])

    cos = cos_ref[...]
    sin = sin_ref[...]
    ri = lax.broadcasted_iota(jnp.int32, (rows, rows), 0)
    ci = lax.broadcasted_iota(jnp.int32, (rows, rows), 1)
    same = (ri >= ci) & ((ri >> tsh) == (ci >> tsh))
    for hd in range(ret_heads):
        lg = log_gamma[hd]
        q = _rotary(proj_scr[:, rq0 + hd * rdk: rq0 + (hd + 1) * rdk], cos, sin)
        k = _rotary(proj_scr[:, rk0 + hd * rdk: rk0 + (hd + 1) * rdk], cos, sin) * (rdk ** -0.5)
        v = proj_scr[:, rv0 + hd * rdv: rv0 + (hd + 1) * rdv].astype(BF16)
        qrot_scr[:, hd * rdk:(hd + 1) * rdk] = q
        krot_scr[:, hd * rdk:(hd + 1) * rdk] = k
        dmat = jnp.where(same, jnp.exp(jnp.where(same, (ri - ci).astype(F32) * lg, 0.0)), 0.0)
        scores = (_dot_nt(q.astype(BF16), k.astype(BF16)) * dmat).astype(BF16)
        oret_scr[:, hd * rdv:(hd + 1) * rdv] = _dot(scores, v)

    ga = proj_scr[:, ga0:ga0 + LANES].astype(BF16)
    logit = _dot(ga, gup_ref[...]) + gbias_ref[...]
    la = _log_sigmoid(logit) * (1.0 / GLA_GATE_NORM)
    tril = jnp.where(same, 1.0, 0.0).astype(BF16)
    la_hi, la_mid, la_lo = _split3(la)
    bcum = _dot(tril, la_hi) + _dot(tril, la_mid) + _dot(tril, la_lo)
    bc_scr[...] = bcum
    c = min(GLA_CHUNK, rows)
    levels = [s for s in (32, 16, 8, 4, 2, 1, 0) if s < ts]
    masks = _gla_level_masks(c, gla_heads, levels)
    hm_k = _head_masks(c, gla_heads, gdk)
    hm_v = _head_masks(c, gla_heads, gdv)
    for ch in range(rows // c):
        r0 = ch * c
        q = proj_scr[r0:r0 + c, gq0:gq0 + gw] * (gdk ** -0.5)
        k = proj_scr[r0:r0 + c, gk0:gk0 + gw]
        v = proj_scr[r0:r0 + c, gv0:gv0 + gvw].astype(BF16)
        bc = bcum[r0:r0 + c]
        scores = _gla_intra_scores(q, k, bc, levels, masks, hm_k)
        ogla_scr[r0:r0 + c, :] = _dot(scores.astype(BF16), _block_diag_rows(v, hm_v))
        gqe_scr[r0:r0 + c, :] = q * jnp.exp(bc)

    trow = lax.broadcasted_iota(jnp.int32, (ts, rdk), 0).astype(F32)
    eye = lax.broadcasted_iota(jnp.int32, (gw, gw), 0) == lax.broadcasted_iota(jnp.int32, (gw, gw), 1)
    zero_blk = jnp.zeros((gdk, gdv), BF16)

    def element(b, carry):
        rs = pl.ds(pl.multiple_of(b * ts, ts), ts)
        for hd in range(ret_heads):
            lg = log_gamma[hd]
            q = qrot_scr[rs, hd * rdk:(hd + 1) * rdk]
            k = krot_scr[rs, hd * rdk:(hd + 1) * rdk]
            v = proj_scr[rs, rv0 + hd * rdv: rv0 + (hd + 1) * rdv]
            s_old = sret_in_ref[b, hd]
            oret_scr[rs, hd * rdv:(hd + 1) * rdv] += jnp.exp((trow + 1.0) * lg) * _dot(
                q.astype(BF16), s_old.astype(BF16))
            k_dec = (k * jnp.exp((ts - 1.0 - trow) * lg)).astype(BF16)
            sret_ref[b, hd] = s_old * math.exp(ts * lg) + _dot_tn(k_dec, v.astype(BF16))
        s_b = sgla_in_ref[b]
        s_bd = jnp.concatenate(
            [jnp.concatenate([s_b[hd].astype(BF16) if h2 == hd else zero_blk for h2 in range(gla_heads)], axis=1)
             for hd in range(gla_heads)], axis=0)
        ogla_scr[rs, :] += _dot(gqe_scr[rs, :].astype(BF16), s_bd)
        bc = bc_scr[rs, :]
        b_last = bc[ts - 1:ts, :]
        k_dec = (proj_scr[rs, gk0:gk0 + gw] * jnp.exp(b_last - bc)).astype(BF16)
        upd = _dot_tn(k_dec, proj_scr[rs, gv0:gv0 + gvw].astype(BF16))
        decay_col = jnp.sum(jnp.where(eye, jnp.broadcast_to(jnp.exp(b_last), (gw, gw)), 0.0), axis=1, keepdims=True)
        for hd in range(gla_heads):
            sgla_ref[b, hd] = (s_b[hd] * decay_col[hd * gdk:(hd + 1) * gdk]
                               + upd[hd * gdk:(hd + 1) * gdk, hd * gdv:(hd + 1) * gdv])
        return carry

    lax.fori_loop(0, nb, element, 0)

    for hd in range(ret_heads):
        o = oret_scr[:, hd * rdv:(hd + 1) * rdv]
        g = proj_scr[:, rg0 + hd * rdv: rg0 + (hd + 1) * rdv]
        mu = jnp.mean(o, axis=-1, keepdims=True)
        oc = o - mu
        var = jnp.mean(oc * oc, axis=-1, keepdims=True)
        y = oc * lax.rsqrt(var + EPS) * rnw_ref[:, hd * rdv:(hd + 1) * rdv] + rnb_ref[:, hd * rdv:(hd + 1) * rdv]
        mix_scr[:, hd * rdv:(hd + 1) * rdv] = (_silu(g) * y).astype(BF16)
    for hd in range(gla_heads):
        oh = ogla_scr[:, hd * gdv:(hd + 1) * gdv]
        gg = proj_scr[:, gg0 + hd * gdv: gg0 + (hd + 1) * gdv]
        y = oh * lax.rsqrt(jnp.mean(oh * oh, axis=-1, keepdims=True) + EPS) * gnw_ref[:, hd * gdv:(hd + 1) * gdv]
        mix_scr[:, ret_heads * rdv + hd * gdv: ret_heads * rdv + (hd + 1) * gdv] = (_silu(gg) * y).astype(BF16)

    x1, h2, logits_t = _mixer_tail(x, mix_scr[...], wout_ref[...], g1, npost_ref[...], npre2_ref[...],
                                   sc2, sh2, wrhi_ref[...], wrlo_ref[...])
    x1_ref[...] = x1
    h2_ref[...] = h2
    lg_ref[...] = logits_t
    _store_token_tiles(h3_ref, h2)


def _sample_mixer(x, mod_t, p, cos, sin, state_ret, state_gla, x1_buf, h2_buf, lg_buf, h3_buf, *, ts, tile):
    n, d = x.shape
    off = (x1_buf.shape[0] - n) // tile
    nb = tile // ts
    rh, gh = p["ret_heads"], p["gla_heads"]
    in_w = p["w_in"].shape[1]
    ne = p["wr_hi"].shape[0]
    gw = p["gup"].shape[1]
    gdk = gw // gh
    full = lambda a: pl.BlockSpec(a.shape, lambda i: (0,) * a.ndim)
    vecs = [p["npre"], p["npost"], p["npre2"], p["gbias"], p["rnw"], p["rnb"], p["gnw"]]
    mats = [p["w_in"], p["gup"], p["w_out"], p["wr_hi"], p["wr_lo"]]
    kern = functools.partial(_sample_mixer_kernel, ret_heads=rh, gla_heads=gh, ts=ts)
    n_in = 2 + len(vecs) + len(mats) + 4 + 4
    return pl.pallas_call(
        kern,
        grid=(n // tile,),
        in_specs=[pl.BlockSpec((tile, d), lambda i: (i, 0)),
                  pl.BlockSpec((6, nb, d), lambda i: (0, i, 0))]
                 + [full(a) for a in vecs] + [full(a) for a in mats]
                 + [full(cos), full(sin),
                    pl.BlockSpec((nb, rh, 128, 128), lambda i: (i, 0, 0, 0)),
                    pl.BlockSpec((nb, gh, gdk, 128), lambda i: (i, 0, 0, 0)),
                    pl.BlockSpec(memory_space=pl.ANY), pl.BlockSpec(memory_space=pl.ANY),
                    pl.BlockSpec(memory_space=pl.ANY), pl.BlockSpec(memory_space=pl.ANY)],
        out_specs=[pl.BlockSpec((tile, d), lambda i: (off + i, 0)),
                   pl.BlockSpec((tile, d), lambda i: (off + i, 0)),
                   pl.BlockSpec((ne, tile), lambda i: (0, off + i)),
                   pl.BlockSpec((tile * (d // LANES), LANES), lambda i: (off + i, 0)),
                   pl.BlockSpec((nb, rh, 128, 128), lambda i: (i, 0, 0, 0)),
                   pl.BlockSpec((nb, gh, gdk, 128), lambda i: (i, 0, 0, 0))],
        out_shape=[jax.ShapeDtypeStruct(x1_buf.shape, F32),
                   jax.ShapeDtypeStruct(h2_buf.shape, F32),
                   jax.ShapeDtypeStruct(lg_buf.shape, F32),
                   jax.ShapeDtypeStruct(h3_buf.shape, F32),
                   jax.ShapeDtypeStruct(state_ret.shape, F32),
                   jax.ShapeDtypeStruct(state_gla.shape, F32)],
        input_output_aliases={n_in - 4: 0, n_in - 3: 1, n_in - 2: 2, n_in - 1: 3},
        scratch_shapes=[pltpu.VMEM((tile, in_w), F32),
                        pltpu.VMEM((tile, p["w_out"].shape[0]), BF16),
                        pltpu.VMEM((tile, rh * 128), F32),
                        pltpu.VMEM((tile, rh * 128), F32),
                        pltpu.VMEM((tile, rh * 128), F32),
                        pltpu.VMEM((tile, gw), F32),
                        pltpu.VMEM((tile, gw), F32),
                        pltpu.VMEM((tile, gh * 128), F32)],
        compiler_params=pltpu.CompilerParams(
            dimension_semantics=("arbitrary",), vmem_limit_bytes=VMEM_LIMIT_BYTES),
        name="sample_mixer",
    )(x, mod_t, *vecs, *mats, cos, sin, state_ret, state_gla, x1_buf, h2_buf, lg_buf, h3_buf)


def _router_kernel(lg_ref, bias_ref, idx_ref, w_ref, cnt_ref, cnt_scr, *, n_groups, topk_groups, top_k):
    ne, tn = lg_ref.shape
    gsz = ne // n_groups
    neg = -jnp.inf

    @pl.when(pl.program_id(1) == 0)
    def _():
        cnt_scr[...] = jnp.zeros_like(cnt_scr)

    scores = jax.nn.sigmoid(lg_ref[...])
    sel = scores + bias_ref[...][:, 0:1]
    sel3 = sel.reshape(n_groups, gsz, tn)
    mem = lax.broadcasted_iota(jnp.int32, (n_groups, gsz, tn), 1)
    m1 = jnp.max(sel3, axis=1, keepdims=True)
    first = jnp.min(jnp.where(sel3 == m1, mem, gsz), axis=1, keepdims=True)
    m2 = jnp.max(jnp.where(mem == first, neg, sel3), axis=1, keepdims=True)
    gscore = (m1 + m2).reshape(n_groups, tn)
    gi = lax.broadcasted_iota(jnp.int32, (n_groups, tn), 0)
    gsel = jnp.zeros((n_groups, tn), jnp.bool_)
    work = gscore
    for _ in range(topk_groups):
        mx = jnp.max(work, axis=0, keepdims=True)
        pick = gi == jnp.min(jnp.where(work == mx, gi, n_groups), axis=0, keepdims=True)
        gsel = jnp.logical_or(gsel, pick)
        work = jnp.where(pick, neg, work)
    emask = jnp.broadcast_to(gsel[:, None, :], (n_groups, gsz, tn)).reshape(ne, tn)
    ei = lax.broadcasted_iota(jnp.int32, (ne, tn), 0)
    work = jnp.where(emask, sel, neg)
    chosen_any = jnp.zeros((ne, tn), jnp.bool_)
    idx_rows, w_rows = [], []
    for _ in range(top_k):
        mx = jnp.max(work, axis=0, keepdims=True)
        first_e = jnp.min(jnp.where(work == mx, ei, ne), axis=0, keepdims=True)
        pick = ei == first_e
        chosen_any = jnp.logical_or(chosen_any, pick)
        work = jnp.where(pick, neg, work)
        idx_rows.append(first_e)
        w_rows.append(jnp.sum(jnp.where(pick, scores, 0.0), axis=0, keepdims=True))
    wsum = w_rows[0]
    for r in w_rows[1:]:
        wsum = wsum + r
    idx_ref[...] = jnp.concatenate(idx_rows, axis=0)
    w_ref[...] = jnp.concatenate(w_rows, axis=0) / wsum * ROUTED_SCALE
    cnt_scr[...] += _dot(jnp.where(chosen_any, 1.0, 0.0).astype(BF16), jnp.ones((tn, LANES), BF16))
    cnt_ref[0] = cnt_scr[...]


def _router(logits_t, bias, *, parts, tile):
    ne, n = logits_t.shape
    tiles = n // parts // tile
    kern = functools.partial(_router_kernel, n_groups=N_GROUPS, topk_groups=TOPK_GROUPS, top_k=TOP_K)
    return pl.pallas_call(
        kern,
        grid=(parts, tiles),
        in_specs=[pl.BlockSpec((ne, tile), lambda p, j: (0, p * tiles + j)),
                  pl.BlockSpec((ne, LANES), lambda p, j: (0, 0))],
        out_specs=[pl.BlockSpec((TOP_K, tile), lambda p, j: (0, p * tiles + j)),
                   pl.BlockSpec((TOP_K, tile), lambda p, j: (0, p * tiles + j)),
                   pl.BlockSpec((1, ne, LANES), lambda p, j: (p, 0, 0))],
        out_shape=[jax.ShapeDtypeStruct((TOP_K, n), jnp.int32),
                   jax.ShapeDtypeStruct((TOP_K, n), F32),
                   jax.ShapeDtypeStruct((parts, ne, LANES), F32)],
        scratch_shapes=[pltpu.VMEM((ne, LANES), F32)],
        compiler_params=pltpu.CompilerParams(
            dimension_semantics=("arbitrary", "arbitrary"), vmem_limit_bytes=VMEM_LIMIT_BYTES),
        name="router_topk",
    )(logits_t, jnp.broadcast_to(bias.reshape(ne, 1), (ne, LANES)))


def _dispatch_plan(idx_t, w_t, counts, *, parts, blk, rows_per_token):
    k, n = idx_t.shape
    npart = n // parts
    ne = counts.shape[1]
    stride = npart + blk
    big = ne * stride
    local = jnp.arange(n, dtype=jnp.int32) % npart
    to_parts = lambda a: a.reshape(k, parts, npart).transpose(1, 0, 2).reshape(parts, k * npart)
    keys = to_parts(idx_t * stride + local[None, :])
    wts = to_parts(w_t)
    need = (-counts) % blk
    j = jnp.arange(blk - 1, dtype=jnp.int32)
    pad_keys = jnp.where(j[None, None, :] < need[:, :, None],
                         jnp.arange(ne, dtype=jnp.int32)[None, :, None] * stride + npart + j[None, None, :], big)
    total = k * npart + ne * (blk - 1)
    nblk = -(-total // blk)
    fill = nblk * blk - total
    all_keys = jnp.concatenate([keys, pad_keys.reshape(parts, -1), jnp.full((parts, fill), big, jnp.int32)], axis=1)
    all_w = jnp.concatenate([wts, jnp.zeros((parts, nblk * blk - k * npart), F32)], axis=1)
    sk, sw = lax.sort((all_keys, all_w), dimension=1, num_keys=1)
    valid = sk < big
    slot_tok = sk % stride
    real = jnp.logical_and(valid, slot_tok < npart)
    acc_row = jnp.where(real, slot_tok, npart) * rows_per_token
    src_row = jnp.where(real, slot_tok, 0) * rows_per_token
    w = jnp.where(real, sw, 0.0)
    blocks_per_expert = (counts + blk - 1) // blk
    first_blk = jnp.concatenate([jnp.zeros((parts, 1), jnp.int32),
                                 jnp.cumsum(blocks_per_expert, axis=1, dtype=jnp.int32)], axis=1)
    pad_src = jnp.zeros((parts, 2 * blk), jnp.int32)
    pad_acc = jnp.full((parts, 2 * blk), npart * rows_per_token, jnp.int32)
    pad_w = jnp.zeros((parts, 2 * blk), F32)
    src_t = jnp.concatenate([src_row, pad_src], axis=1)
    acc_t = jnp.concatenate([pad_acc, acc_row], axis=1)
    w_t2 = jnp.concatenate([pad_w, w], axis=1)
    return src_t.reshape(-1), acc_t.reshape(-1), w_t2.reshape(-1), first_blk, nblk


def _moe_kernel(first_ref, src_hbm, dst_hbm, wts_hbm, h3_ref, h2_ref, x1_ref, g2_ref, npost_ref,
                wg_ref, wu_ref, wd_ref, wsg_ref, wsu_ref, wsd_ref, ya_ref, yb_ref,
                acc_scr, g_scr, y2_scr, xs_scr, wgu_scr, wdn_scr, src_smem, dst_smem, wts_smem, sem,
                *, ne, nblk, blk, ts, n_epi, tiles_a):
    p = pl.program_id(0)
    b = pl.program_id(1)
    d = h2_ref.shape[1]
    ff = wg_ref.shape[2]
    nchunk = d // LANES
    tile_at = lambda row: pl.ds(pl.multiple_of(row, nchunk), nchunk)

    depth = src_smem.shape[0]
    n_valid = first_ref[p, ne]
    last_t = n_valid + 1

    def table_copies(t):
        slot = t & (depth - 1)
        row = pl.multiple_of((p * (nblk + 2) + t) * blk, blk)
        return tuple(pltpu.make_async_copy(hbm.at[pl.ds(row, blk)], smem.at[slot], sem.at[i, slot])
                     for i, (hbm, smem) in enumerate(((src_hbm, src_smem), (dst_hbm, dst_smem), (wts_hbm, wts_smem))))

    def gather(slot, xs_ref):
        for r in range(blk):
            g_scr[r * nchunk:(r + 1) * nchunk, :] = h3_ref[tile_at(src_smem[slot, r]), :]
        for c in range(nchunk):
            xs_ref[:, c * LANES:(c + 1) * LANES] = g_scr[pl.ds(c, blk, stride=nchunk), :].astype(BF16)

    def expert_ffn(xs_ref, y2_ref):
        gu = _dot(xs_ref[...], wgu_scr[...])
        hid = (_silu(gu[:, :ff]) * gu[:, ff:]).astype(BF16)
        y = _dot(hid, wdn_scr[...])
        for c in range(nchunk):
            y2_ref[pl.ds(c, blk, stride=nchunk), :] = y[:, c * LANES:(c + 1) * LANES]

    def scatter_add(slot, y2_ref):
        for g in range(blk // SUBLANES):
            rows = [g * SUBLANES + i for i in range(SUBLANES)]
            dsts = [dst_smem[slot, r] for r in rows]
            new = [acc_scr[tile_at(t), :] + wts_smem[slot, r] * y2_ref[r * nchunk:(r + 1) * nchunk, :]
                   for t, r in zip(dsts, rows)]
            for t, v in zip(dsts, new):
                acc_scr[tile_at(t), :] = v

    @pl.when(b == 0)
    def _():
        acc_scr[...] = jnp.zeros_like(acc_scr)
        y2_scr[...] = jnp.zeros_like(y2_scr)
        for t in range(depth):
            @pl.when(t <= last_t)
            def _():
                for cp in table_copies(t):
                    cp.start()
        for cp in table_copies(0):
            cp.wait()
        gather(0, xs_scr.at[0])

    def pipeline_step(f, carry):
        t = f + 1
        slot = t & (depth - 1)
        for cp in table_copies(t):
            cp.wait()

        @pl.when(t + (depth - 1) <= last_t)
        def _():
            for cp in table_copies(t + (depth - 1)):
                cp.start()

        for parity in range(2):
            @pl.when((t & 1) == parity)
            def _():
                gather(slot, xs_scr.at[parity])
                expert_ffn(xs_scr.at[1 - parity], y2_scr.at[1 - parity])
                scatter_add(slot, y2_scr.at[parity])
        return carry

    @pl.when(b < ne)
    def _():
        e = jnp.minimum(b, ne - 1)
        f0, f1 = first_ref[p, e], first_ref[p, e + 1]

        @pl.when(f1 > f0)
        def _():
            wgu_scr[:, :ff] = wg_ref[0].astype(BF16)
            wgu_scr[:, ff:] = wu_ref[0].astype(BF16)
            wdn_scr[...] = wd_ref[0].astype(BF16)
            lax.fori_loop(f0, f1, pipeline_step, 0)

    @pl.when(b == ne)
    def _():
        for cp in table_copies(last_t):
            cp.wait()
        for parity in range(2):
            @pl.when((last_t & 1) == parity)
            def _():
                scatter_add(last_t & (depth - 1), y2_scr.at[parity])

    @pl.when(b >= ne)
    def _():
        rows = ya_ref.shape[0]
        t0 = pl.multiple_of((b - ne) * rows, rows)
        routed = jnp.concatenate(
            [acc_scr[pl.ds(t0 * nchunk + c, rows, stride=nchunk), :] for c in range(nchunk)], axis=1)
        hs = h2_ref[...].astype(BF16)
        hid = (_silu(_dot(hs, wsg_ref[...])) * _dot(hs, wsu_ref[...])).astype(BF16)
        f = routed + _dot(hid, wsd_ref[...])
        g2 = g2_ref[...].reshape(rows // ts, d)
        g2 = jnp.broadcast_to(g2[:, None, :], (rows // ts, ts, d)).reshape(rows, d)
        y = x1_ref[...] + g2 * _rms(f, npost_ref[...])
        tile = p * n_epi + (b - ne)

        @pl.when(tile < tiles_a)
        def _():
            ya_ref[...] = y

        @pl.when(tile >= tiles_a)
        def _():
            yb_ref[...] = y


def _moe(h2, h3, x1, g2_tiles, npost, plan, w_gate, w_up, w_down, wsg, wsu, wsd, *, parts, blk, ts, n_a):
    src_row, acc_row, wts, first_blk, nblk = plan
    n, d = h2.shape
    npart = n // parts
    ne, _, ff = w_gate.shape
    epi = MOE_EPILOGUE_TILE
    n_epi = npart // epi
    tiles_a = n_a // epi
    tiles_b = (n - n_a) // epi
    nchunk = d // LANES
    kern = functools.partial(_moe_kernel, ne=ne, nblk=nblk, blk=blk, ts=ts, n_epi=n_epi, tiles_a=tiles_a)
    epi_of = lambda p, b: p * n_epi + jnp.clip(b - ne, 0, n_epi - 1)
    exp_of = lambda p, b: jnp.minimum(b, ne - 1)
    full = lambda a: pl.BlockSpec(a.shape, lambda p, b, fb: (0,) * a.ndim)
    grid_spec = pltpu.PrefetchScalarGridSpec(
        num_scalar_prefetch=1,
        grid=(parts, ne + n_epi),
        in_specs=[
            pl.BlockSpec(memory_space=pl.ANY),
            pl.BlockSpec(memory_space=pl.ANY),
            pl.BlockSpec(memory_space=pl.ANY),
            pl.BlockSpec((npart * nchunk, LANES), lambda p, b, fb: (p, 0), pipeline_mode=pl.Buffered(1)),
            pl.BlockSpec((epi, d), lambda p, b, fb: (epi_of(p, b), 0)),
            pl.BlockSpec((epi, d), lambda p, b, fb: (epi_of(p, b), 0)),
            pl.BlockSpec((epi // LANES, LANES // ts, d), lambda p, b, fb: (epi_of(p, b), 0, 0)),
            full(npost),
            pl.BlockSpec((1, d, ff), lambda p, b, fb: (exp_of(p, b), 0, 0)),
            pl.BlockSpec((1, d, ff), lambda p, b, fb: (exp_of(p, b), 0, 0)),
            pl.BlockSpec((1, ff, d), lambda p, b, fb: (exp_of(p, b), 0, 0)),
            full(wsg), full(wsu), full(wsd),
        ],
        out_specs=[
            pl.BlockSpec((epi, d), lambda p, b, fb: (jnp.minimum(epi_of(p, b), tiles_a - 1), 0)),
            pl.BlockSpec((epi, d), lambda p, b, fb: (jnp.clip(epi_of(p, b) - tiles_a, 0, tiles_b - 1), 0))],
        scratch_shapes=[pltpu.VMEM(((npart + 1) * (d // LANES), LANES), F32),
                        pltpu.VMEM((blk * (d // LANES), LANES), F32),
                        pltpu.VMEM((2, blk * (d // LANES), LANES), F32),
                        pltpu.VMEM((2, blk, d), BF16),
                        pltpu.VMEM((d, 2 * ff), BF16),
                        pltpu.VMEM((ff, d), BF16),
                        pltpu.SMEM((MOE_TABLE_DEPTH, blk), jnp.int32),
                        pltpu.SMEM((MOE_TABLE_DEPTH, blk), jnp.int32),
                        pltpu.SMEM((MOE_TABLE_DEPTH, blk), F32),
                        pltpu.SemaphoreType.DMA((3, MOE_TABLE_DEPTH))],
    )
    return pl.pallas_call(
        kern,
        grid_spec=grid_spec,
        out_shape=[jax.ShapeDtypeStruct((n_a, d), F32), jax.ShapeDtypeStruct((n - n_a, d), F32)],
        compiler_params=pltpu.CompilerParams(
            dimension_semantics=("arbitrary", "arbitrary"), vmem_limit_bytes=MOE_VMEM_LIMIT_BYTES),
        name="moe_experts",
    )(first_blk, src_row, acc_row, wts, h3, h2, x1, g2_tiles, npost, w_gate, w_up, w_down, wsg, wsu, wsd)


def _rope_tables(pos):
    half = 64
    inv_freq = ROPE_BASE ** (-jnp.arange(half, dtype=F32) / half)
    ang = pos[:, None] * inv_freq[None, :]
    cos, sin = jnp.cos(ang), jnp.sin(ang)
    return jnp.concatenate([cos, cos], axis=1), jnp.concatenate([-sin, sin], axis=1)


def _mixer_params(norm_pre_mix, norm_post_mix, norm_pre_ffn, w_in, gla_gate_up, gla_gate_bias,
                  ret_norm_w, ret_norm_b, gla_norm_w, w_out, w_router):
    d, in_w = w_in.shape
    pad = (-in_w) % LANES
    rank = gla_gate_up.shape[0]
    wr_hi, wr_lo = _split2(w_router.T)
    return dict(
        ret_heads=ret_norm_w.shape[0], gla_heads=gla_norm_w.shape[0],
        npre=norm_pre_mix.reshape(1, d), npost=norm_post_mix.reshape(1, d), npre2=norm_pre_ffn.reshape(1, d),
        gbias=gla_gate_bias.reshape(1, -1), rnw=ret_norm_w.reshape(1, -1), rnb=ret_norm_b.reshape(1, -1),
        gnw=gla_norm_w.reshape(1, -1),
        w_in=jnp.pad(w_in, ((0, 0), (0, pad))).astype(BF16),
        gup=jnp.pad(gla_gate_up, ((0, LANES - rank), (0, 0))).astype(BF16),
        w_out=w_out.astype(BF16), wr_hi=wr_hi, wr_lo=wr_lo)


def kernel(x_prompt, x_sample, state_ret, state_gla, c_prompt, c_sample, w_ada, b_ada, norm_pre_mix, norm_post_mix, norm_pre_ffn, norm_post_ffn, w_in, gla_gate_up, gla_gate_bias, ret_norm_w, ret_norm_b, gla_norm_w, w_out, w_router, router_bias, w_exp_gate, w_exp_up, w_exp_down, w_sh_gate, w_sh_up, w_sh_down):
    bp, tp, d = x_prompt.shape
    bs, ts, _ = x_sample.shape
    l = 0
    c_all = jnp.concatenate([c_prompt, c_sample], axis=0)
    mod = _ada(c_all, w_ada[l], b_ada[l]).reshape(bp + bs, 6, d)
    p = _mixer_params(norm_pre_mix[l], norm_post_mix[l], norm_pre_ffn[l], w_in[l], gla_gate_up[l],
                      gla_gate_bias[l], ret_norm_w[l], ret_norm_b[l], gla_norm_w[l], w_out[l], w_router[l])
    n_p, n_s = bp * tp, bs * ts
    n = n_p + n_s
    cos_p, sin_p = _rope_tables(jnp.arange(tp, dtype=F32))
    x1, h2, logits_t, h3, sret_p, sgla_p = _prompt_mixer(x_prompt, mod[:bp], p, cos_p, sin_p,
                                                         tile=PROMPT_TILE, n_total=n)
    cos_s, sin_s = _rope_tables(PAST_LEN + jnp.arange(ts, dtype=F32))
    cos_s, sin_s = jnp.tile(cos_s, (SAMPLE_TILE // ts, 1)), jnp.tile(sin_s, (SAMPLE_TILE // ts, 1))
    x1, h2, logits_t, h3, sret_s, sgla_s = _sample_mixer(
        x_sample.reshape(n_s, d), mod[bp:].transpose(1, 0, 2), p, cos_s, sin_s, state_ret[l], state_gla[l],
        x1, h2, logits_t, h3, ts=ts, tile=SAMPLE_TILE)
    gh, gdk = p["gla_heads"], p["gup"].shape[1] // p["gla_heads"]
    sgla_blocks = sgla_p.reshape(bp, gh, 128, gh, gdk)
    new_gla_p = jnp.stack([sgla_blocks[:, h, :, h, :] for h in range(gh)], axis=1).transpose(0, 1, 3, 2)

    idx_t, w_t, cnt = _router(logits_t, router_bias[l], parts=MOE_PARTS, tile=ROUTER_TILE)
    plan = _dispatch_plan(idx_t, w_t, cnt[:, :, 0].astype(jnp.int32), parts=MOE_PARTS, blk=MOE_BLOCK,
                          rows_per_token=d // LANES)
    per_tile = LANES // ts
    g2 = mod[:, 5, :]
    g2_tiles = jnp.concatenate(
        [jnp.broadcast_to(jnp.repeat(g2[:bp], tp // LANES, axis=0)[:, None, :], (n_p // LANES, per_tile, d)),
         g2[bp:].reshape(n_s // LANES, per_tile, d)], axis=0)
    y_p, y_s = _moe(h2, h3, x1, g2_tiles, norm_post_ffn[l].reshape(1, d), plan,
                    w_exp_gate[l], w_exp_up[l], w_exp_down[l],
                    w_sh_gate[l].astype(BF16), w_sh_up[l].astype(BF16), w_sh_down[l].astype(BF16),
                    parts=MOE_PARTS, blk=MOE_BLOCK, ts=ts, n_a=n_p)
    return (y_p.reshape(bp, tp, d), y_s.reshape(bs, ts, d), sret_p[None], new_gla_p[None],
            sret_s[None], sgla_s[None])
```
